```python
import math
import jax
import jax.numpy as jnp
from jax import lax
import numpy as np


D_MODEL = 1024
BATCH = 4
SEQ = 4096
DEPTH = 2

GRID_W = 64
CTX_LEN = 256
HEAD_DIM = 64
N_GROUP_HEADS = 4
GROUP_WIDTH = N_GROUP_HEADS * HEAD_DIM
N_MIXERS = 4
MIX_WIDTH = N_MIXERS * GROUP_WIDTH
A_K_OFF = 1 * GROUP_WIDTH
A_V_OFF = 2 * GROUP_WIDTH
POOL_OFF = 3 * GROUP_WIDTH
SGU_OFF = 4 * GROUP_WIDTH
N_Q_OFF = 6 * GROUP_WIDTH
N_K_OFF = 7 * GROUP_WIDTH
N_V_OFF = 8 * GROUP_WIDTH
PROJ_WIDTH = 9 * GROUP_WIDTH
PROJ_SPLITS = [A_K_OFF, A_V_OFF, POOL_OFF, SGU_OFF, N_Q_OFF, N_K_OFF, N_V_OFF]
DA_QK = HEAD_DIM // 2
ROPE_BASE = 10000.0
Q_BLOCK = 128
POOL_WINDOWS = (2, 4, 8, 16)
POOL_CH = GROUP_WIDTH // len(POOL_WINDOWS)
SGU_CHUNK = 128
WIN_R = 8
WIN_C = 16
NA_QCOLS = 16
NA_SPAN = 2 * WIN_C
N_EXPERTS = 32
TOP_K = 4
D_EXPERT = D_MODEL
SWIGLU_LIMIT = 7.0
SWIGLU_ALPHA = 1.702
MOE_BLOCK = 256
EPS = 1e-6
NEG_INF = -1e30

kernel_name = 'hybrid_parallel_heads_diffusion_block'


def rms_norm(x, g):
    xf = x.astype(jnp.float32)
    y = xf * lax.rsqrt(jnp.mean(xf * xf, axis=-1, keepdims=True) + EPS)
    return (y * g.astype(jnp.float32)).astype(x.dtype)


def _heads(t):
    return None if t is None else t.reshape(t.shape[:-1] + (N_GROUP_HEADS, HEAD_DIM))


def rope_1d(x, pos):
    n = x.shape[-1] // 2
    inv = ROPE_BASE ** (-jnp.arange(n, dtype=jnp.float32) / n)
    ang = pos.astype(jnp.float32)[:, None] * inv[None, :]
    cos = jnp.cos(ang)[None, :, None, :].astype(x.dtype)
    sin = jnp.sin(ang)[None, :, None, :].astype(x.dtype)
    x1, x2 = x[..., :n], x[..., n:]
    return jnp.concatenate([x1 * cos - x2 * sin, x1 * sin + x2 * cos], axis=-1)


def axial_rope(x, row_pos, col_pos):
    h = x.shape[-1] // 2
    return jnp.concatenate([rope_1d(x[..., :h], row_pos), rope_1d(x[..., h:], col_pos)], axis=-1)


def diff_attention(q, k, v, q_ctx, k_ctx, v_ctx, lam_q1, lam_k1, lam_q2, lam_k2, g_sub,
                   lam_init, row_pos, col_pos):
    B, S, H, Dv = v.shape
    lam = (jnp.exp(jnp.sum(lam_q1.astype(jnp.float32) * lam_k1.astype(jnp.float32)))
           - jnp.exp(jnp.sum(lam_q2.astype(jnp.float32) * lam_k2.astype(jnp.float32)))
           + lam_init)
    scale = DA_QK ** -0.5

    def two_maps(qq, kk, vv):
        s1 = jnp.einsum('bqhd,bkhd->bhqk', qq[..., :DA_QK], kk[..., :DA_QK]).astype(jnp.float32) * scale
        s2 = jnp.einsum('bqhd,bkhd->bhqk', qq[..., DA_QK:], kk[..., DA_QK:]).astype(jnp.float32) * scale
        a = jax.nn.softmax(s1, axis=-1) - lam * jax.nn.softmax(s2, axis=-1)
        o = jnp.einsum('bhqk,bkhd->bqhd', a, vv)
        return (rms_norm(o, g_sub) * (1.0 - lam_init)).astype(vv.dtype)

    def rot(t):
        return jnp.concatenate([axial_rope(t[..., :DA_QK], row_pos, col_pos),
                                axial_rope(t[..., DA_QK:], row_pos, col_pos)], axis=-1)

    k_all = jnp.concatenate([rot(k), k_ctx], axis=1)
    v_all = jnp.concatenate([v, v_ctx], axis=1)
    q_blocks = jnp.moveaxis(rot(q).reshape(B, S // Q_BLOCK, Q_BLOCK, H, 2 * DA_QK), 1, 0)
    o = lax.map(lambda qb: two_maps(qb, k_all, v_all), q_blocks)
    o = jnp.moveaxis(o, 0, 1).reshape(B, S, H * Dv)
    o_ctx = None if q_ctx is None else two_maps(q_ctx, k_ctx, v_ctx).reshape(B, -1, H * Dv)
    return o, o_ctx


def ctx_attention(q, k, v):
    s = jnp.einsum('bqhd,bkhd->bhqk', q, k).astype(jnp.float32) * q.shape[-1] ** -0.5
    o = jnp.einsum('bhqk,bkhd->bqhd', jax.nn.softmax(s, axis=-1), v).astype(q.dtype)
    return o.reshape(q.shape[0], q.shape[1], -1)


def neighbourhood_attention(q, k, v, q_ctx, k_ctx, v_ctx, rpb):
    B, S, H, Dh = q.shape
    rows = S // GRID_W
    kr = min(WIN_R, rows)
    ncb = GRID_W // NA_QCOLS
    scale = Dh ** -0.5
    r = jnp.arange(rows)
    blk = jnp.arange(ncb)
    row_idx = jnp.clip(r - kr // 2, 0, rows - kr)[:, None] + jnp.arange(kr)[None, :]
    span_idx = (jnp.clip(blk * NA_QCOLS - WIN_C // 2, 0, GRID_W - NA_SPAN)[:, None]
                + jnp.arange(NA_SPAN)[None, :])
    qcol = blk[:, None] * NA_QCOLS + jnp.arange(NA_QCOLS)[None, :]
    win_start = jnp.clip(qcol - WIN_C // 2, 0, GRID_W - WIN_C)
    kc = span_idx[:, None, :]
    col_ok = (kc >= win_start[..., None]) & (kc < win_start[..., None] + WIN_C)
    mask = jnp.broadcast_to(col_ok[:, :, None, :], (ncb, NA_QCOLS, kr, NA_SPAN))
    mask = mask.reshape(ncb, NA_QCOLS, kr * NA_SPAN)
    rel_r = row_idx - r[:, None] + (WIN_R - 1)
    rel_c = jnp.clip(kc - qcol[..., None], -(WIN_C - 1), WIN_C - 1) + (WIN_C - 1)
    bias = rpb[:, rel_r[:, None, None, :, None], rel_c[None, :, :, None, :]]
    bias = jnp.moveaxis(bias, 0, 2).reshape(rows, ncb, H, NA_QCOLS, kr * NA_SPAN)

    def gather(t):
        tg = t.reshape(B, rows, GRID_W, H, Dh)[:, row_idx[:, None, :, None], span_idx[None, :, None, :]]
        return tg.reshape(B, rows, ncb, kr * NA_SPAN, H, Dh)

    kg, vg = gather(k), gather(v)
    qg = q.reshape(B, rows, ncb, NA_QCOLS, H, Dh)
    s_n = jnp.einsum('brnqhd,brnkhd->brnhqk', qg, kg).astype(jnp.float32) * scale + bias[None]
    s_n = jnp.where(mask[None, None, :, None], s_n, NEG_INF)
    s_c = jnp.einsum('brnqhd,bchd->brnhqc', qg, k_ctx).astype(jnp.float32) * scale
    p = jax.nn.softmax(jnp.concatenate([s_n, s_c], axis=-1), axis=-1)
    nk = kr * NA_SPAN
    o = (jnp.einsum('brnhqk,brnkhd->brnqhd', p[..., :nk], vg)
         + jnp.einsum('brnhqc,bchd->brnqhd', p[..., nk:], v_ctx))
    o = o.astype(q.dtype).reshape(B, S, H * Dh)
    o_ctx = None if q_ctx is None else ctx_attention(q_ctx, k_ctx, v_ctx)
    return o, o_ctx


def pool_mixer(y, w_pool, s_pool):
    L = y.shape[1]
    t = jnp.arange(L)
    yf = y.astype(jnp.float32)
    csum = jnp.concatenate([jnp.zeros_like(yf[:, :1]), jnp.cumsum(yf, axis=1)], axis=1)
    outs = []
    for g, w in enumerate(POOL_WINDOWS):
        sl = slice(g * POOL_CH, (g + 1) * POOL_CH)
        lo = jnp.clip(t - w // 2, 0, L - 1)
        hi = jnp.clip(t + (w - w // 2 - 1), 0, L - 1)
        cnt = (hi - lo + 1).astype(jnp.float32)[None, :, None]
        mean = (csum[:, hi + 1, sl] - csum[:, lo, sl]) / cnt
        outs.append((mean - yf[..., sl]).astype(y.dtype) @ w_pool[g])
    return jnp.concatenate(outs, axis=-1) * s_pool


def sgu_mixer(z, g_sgu, w_sgu, b_sgu):
    B, L, _ = z.shape
    z = jax.nn.gelu(z)
    u, v = z[..., :GROUP_WIDTH], z[..., GROUP_WIDTH:]
    v = rms_norm(v, g_sgu).reshape(B, L // SGU_CHUNK, SGU_CHUNK, N_GROUP_HEADS, HEAD_DIM)
    s = jnp.einsum('gts,bnsgc->bntgc', w_sgu, v) + b_sgu.T[:, :, None]
    return u * s.reshape(B, L, GROUP_WIDTH)


def moe_ffn(h, w_router, b_router, w_gu, b_gu, w_down, b_down):
    T, D = h.shape
    logits = h.astype(jnp.float32) @ w_router.astype(jnp.float32) + b_router.astype(jnp.float32)
    top_val, top_idx = lax.top_k(logits, TOP_K)
    gates = jax.nn.softmax(top_val, axis=-1)
    M = T * TOP_K
    flat_e = top_idx.reshape(-1)
    order = jnp.argsort(flat_e)
    sorted_e = flat_e[order]
    counts = jnp.bincount(flat_e, length=N_EXPERTS)
    padded = (counts + MOE_BLOCK - 1) // MOE_BLOCK * MOE_BLOCK
    start = jnp.cumsum(counts) - counts
    cum_pad = jnp.cumsum(padded)
    pstart = cum_pad - padded
    dest = pstart[sorted_e] + jnp.arange(M) - start[sorted_e]
    nb = -(-M // MOE_BLOCK) + N_EXPERTS
    slot_tok = jnp.zeros((nb * MOE_BLOCK,), jnp.int32).at[dest].set((order // TOP_K).astype(jnp.int32))
    blk_e = jnp.minimum(jnp.searchsorted(cum_pad, jnp.arange(nb) * MOE_BLOCK, side='right'), N_EXPERTS - 1)

    def expert_block(args):
        tok, e = args
        xb = h[tok]
        gu = xb @ w_gu[e] + b_gu[e]
        gt = jnp.minimum(gu[..., :D_EXPERT], SWIGLU_LIMIT)
        up = jnp.clip(gu[..., D_EXPERT:], -SWIGLU_LIMIT, SWIGLU_LIMIT)
        act = (up + 1.0) * gt * jax.nn.sigmoid(SWIGLU_ALPHA * gt)
        return act @ w_down[e] + b_down[e]

    yb = lax.map(expert_block, (slot_tok.reshape(nb, MOE_BLOCK), blk_e)).reshape(nb * MOE_BLOCK, D)
    y = jnp.zeros((M, D), yb.dtype).at[order].set(yb[dest]).reshape(T, TOP_K, D)
    return jnp.einsum('tkd,tk->td', y, gates.astype(y.dtype))


def hybrid_layer(x, xc, c, c_ctx, p, layer_idx, update_ctx):
    B, S, D = x.shape
    t = jnp.arange(S)
    row_pos, col_pos = t // GRID_W, t % GRID_W
    lam_init = 0.8 - 0.6 * math.exp(-0.3 * layer_idx)
    mod = (jax.nn.silu(c) @ p['w_mod'] + p['b_mod'])[:, None, :]
    sh1, sc1, gt1, sh2, sc2, gt2 = jnp.split(mod, 6, axis=-1)
    n_cm = 6 if update_ctx else 2
    modc = jnp.split(jax.nn.silu(c_ctx) @ p['w_mod'][:, :n_cm * D] + p['b_mod'][:n_cm * D], n_cm, axis=-1)

    h = rms_norm(x, p['g_mix']) * (1 + sc1) + sh1
    hc = rms_norm(xc, p['g_mix']) * (1 + modc[1]) + modc[0]
    aq, ak, av, pool_in, sgu_in, nq, nk, nv = jnp.split(h @ p['w_in'], PROJ_SPLITS, axis=-1)
    if update_ctx:
        aqc, akc, avc, pool_c, sgu_c, nqc, nkc, nvc = jnp.split(hc @ p['w_in'], PROJ_SPLITS, axis=-1)
    else:
        akc, avc = jnp.split(hc @ p['w_in'][:, A_K_OFF:POOL_OFF], 2, axis=-1)
        nkc, nvc = jnp.split(hc @ p['w_in'][:, N_K_OFF:], 2, axis=-1)
        aqc = nqc = None
    oa, oa_c = diff_attention(_heads(aq), _heads(ak), _heads(av), _heads(aqc), _heads(akc), _heads(avc),
                              p['lam_q1'], p['lam_k1'], p['lam_q2'], p['lam_k2'], p['g_sub'],
                              lam_init, row_pos, col_pos)
    od, od_c = neighbourhood_attention(_heads(nq), _heads(nk), _heads(nv), _heads(nqc), _heads(nkc),
                                       _heads(nvc), p['rpb'])
    mix = jnp.concatenate([oa, pool_mixer(pool_in, p['w_pool'], p['s_pool']),
                           sgu_mixer(sgu_in, p['g_sgu'], p['w_sgu'], p['b_sgu']), od], axis=-1)
    x = x + gt1 * (mix @ p['w_out'])

    hf = rms_norm(x, p['g_ffn']) * (1 + sc2) + sh2
    if update_ctx:
        mixc = jnp.concatenate([oa_c, pool_mixer(pool_c, p['w_pool'], p['s_pool']),
                                sgu_mixer(sgu_c, p['g_sgu'], p['w_sgu'], p['b_sgu']), od_c], axis=-1)
        xc = xc + modc[2] * (mixc @ p['w_out'])
        hfc = rms_norm(xc, p['g_ffn']) * (1 + modc[4]) + modc[3]
        y = moe_ffn(jnp.concatenate([hf.reshape(-1, D), hfc.reshape(-1, D)], axis=0), p['w_router'],
                    p['b_router'], p['w_gu'], p['b_gu'], p['w_down'], p['b_down'])
        x = x + gt2 * y[:B * S].reshape(B, S, D)
        xc = xc + modc[5] * y[B * S:].reshape(xc.shape)
    else:
        y = moe_ffn(hf.reshape(-1, D), p['w_router'], p['b_router'], p['w_gu'], p['b_gu'],
                    p['w_down'], p['b_down'])
        x = x + gt2 * y.reshape(B, S, D)
    return x, xc


def setup_inputs(seed: int = 0) -> dict:
    key = jax.random.key(seed)
    ks = jax.random.split(key, 32)
    f32 = jnp.float32

    def nrm(i, shape, s):
        return jax.random.normal(ks[i], shape, f32) * s

    L, D, G, E, F = DEPTH, D_MODEL, GROUP_WIDTH, N_EXPERTS, D_EXPERT
    return {
        'x': nrm(0, (BATCH, SEQ, D), 1.0),
        'c': nrm(1, (BATCH, D), 1.0),
        'ctx': nrm(2, (BATCH, CTX_LEN, D), 1.0),
        'c_ctx': nrm(3, (D,), 1.0),
        'w_mod': nrm(4, (L, D, 6 * D), 0.5 * D ** -0.5),
        'b_mod': nrm(5, (L, 6 * D), 0.02),
        'g_mix': 1.0 + nrm(6, (L, D), 0.05),
        'g_ffn': 1.0 + nrm(7, (L, D), 0.05),
        'w_in': nrm(8, (L, D, PROJ_WIDTH), D ** -0.5),
        'w_out': nrm(9, (L, MIX_WIDTH, D), MIX_WIDTH ** -0.5),
        'lam_q1': nrm(10, (L, DA_QK), 0.1),
        'lam_k1': nrm(11, (L, DA_QK), 0.1),
        'lam_q2': nrm(12, (L, DA_QK), 0.1),
        'lam_k2': nrm(13, (L, DA_QK), 0.1),
        'g_sub': 1.0 + nrm(14, (L, HEAD_DIM), 0.05),
        'w_pool': nrm(15, (L, len(POOL_WINDOWS), POOL_CH, POOL_CH), POOL_CH ** -0.5),
        's_pool': 1.0 + nrm(16, (L, G), 0.1),
        'g_sgu': 1.0 + nrm(17, (L, G), 0.05),
        'w_sgu': nrm(18, (L, N_GROUP_HEADS, SGU_CHUNK, SGU_CHUNK), SGU_CHUNK ** -0.5),
        'b_sgu': 1.0 + nrm(19, (L, N_GROUP_HEADS, SGU_CHUNK), 0.1),
        'rpb': nrm(20, (L, N_GROUP_HEADS, 2 * WIN_R - 1, 2 * WIN_C - 1), 0.1),
        'w_router': nrm(21, (L, D, E), D ** -0.5),
        'b_router': nrm(22, (L, E), 0.01),
        'w_gu': nrm(23, (L, E, D, 2 * F), D ** -0.5),
        'b_gu': nrm(24, (L, E, 2 * F), 0.01),
        'w_down': nrm(25, (L, E, F, D), F ** -0.5),
        'b_down': nrm(26, (L, E, D), 0.01),
        'g_final': 1.0 + nrm(27, (D,), 0.05),
    }


def reference(x, c, ctx, c_ctx, w_mod, b_mod, g_mix, g_ffn, w_in, w_out, lam_q1, lam_k1, lam_q2,
              lam_k2, g_sub, w_pool, s_pool, g_sgu, w_sgu, b_sgu, rpb, w_router, b_router, w_gu,
              b_gu, w_down, b_down, g_final):
    xc = ctx
    for l in range(DEPTH):
        p = dict(w_mod=w_mod[l], b_mod=b_mod[l], g_mix=g_mix[l], g_ffn=g_ffn[l], w_in=w_in[l],
                 w_out=w_out[l], lam_q1=lam_q1[l], lam_k1=lam_k1[l], lam_q2=lam_q2[l],
                 lam_k2=lam_k2[l], g_sub=g_sub[l], w_pool=w_pool[l], s_pool=s_pool[l],
                 g_sgu=g_sgu[l], w_sgu=w_sgu[l], b_sgu=b_sgu[l], rpb=rpb[l],
                 w_router=w_router[l], b_router=b_router[l], w_gu=w_gu[l], b_gu=b_gu[l],
                 w_down=w_down[l], b_down=b_down[l])
        x, xc = hybrid_layer(x, xc, c, c_ctx, p, l, l < DEPTH - 1)
    return rms_norm(x, g_final)
```

```python
import functools
import math

import jax
import jax.numpy as jnp
import numpy as np
from jax import lax
from jax.experimental import pallas as pl
from jax.experimental.pallas import tpu as pltpu

F32 = jnp.float32
BF16 = jnp.bfloat16

D_MODEL = 1024
DEPTH = 2
GRID_W = 64
HEAD_DIM = 64
N_HEADS = 4
GROUP_WIDTH = N_HEADS * HEAD_DIM
PROJ_WIDTH = 9 * GROUP_WIDTH
DA_QK = HEAD_DIM // 2
ROPE_BASE = 10000.0
POOL_WINDOWS = (2, 4, 8, 16)
SGU_CHUNK = 128
WIN_R = 8
WIN_C = 16
N_EXPERTS = 32
TOP_K = 4
D_EXPERT = D_MODEL
SWIGLU_LIMIT = 7.0
SWIGLU_ALPHA = 1.702
EPS = 1e-6
NEG_INF = -1e30

COL_AQ, COL_AK, COL_AV, COL_POOL, COL_SGU, COL_NQ, COL_NK, COL_NV = 0, 1, 2, 3, 4, 6, 7, 8

NA_QROWS = 4
NA_BAND = 12
POOL_HALO = 16
MOE_BM = 256
VMEM_LIMIT = 48 * 1024 * 1024


def _cparams(sem):
    return pltpu.CompilerParams(dimension_semantics=sem, vmem_limit_bytes=VMEM_LIMIT)


def _lane_iota(n):
    return lax.broadcasted_iota(jnp.int32, (1, n), 1)


def _dot_nt(a, b):
    return lax.dot_general(a, b, (((1,), (1,)), ((), ())), preferred_element_type=F32)


def _mod_kernel(c_ref, w_ref, b_ref, o_ref):
    c = c_ref[...]
    s = c * (1.0 / (1.0 + jnp.exp(-c)))
    o_ref[...] = jnp.dot(s.astype(BF16), w_ref[...].astype(BF16),
                         preferred_element_type=F32) + b_ref[...]


def _mod_call(c8, w_mod, b_mod):
    n = w_mod.shape[1]
    tn = 1536
    return pl.pallas_call(
        _mod_kernel,
        grid=(n // tn,),
        in_specs=[pl.BlockSpec((8, D_MODEL), lambda j: (0, 0)),
                  pl.BlockSpec((D_MODEL, tn), lambda j: (0, j)),
                  pl.BlockSpec((1, tn), lambda j: (0, j))],
        out_specs=pl.BlockSpec((8, tn), lambda j: (0, j)),
        out_shape=jax.ShapeDtypeStruct((8, n), F32),
        compiler_params=_cparams(("arbitrary",)),
        name="adaln_mod",
    )(c8, w_mod, b_mod.reshape(1, n))


def _swap8(x):
    n = x.shape[-1]
    lane = _lane_iota(n)
    first_half = (lane & 15) < 8
    return jnp.where(first_half, pltpu.roll(x, n - 8, axis=1), pltpu.roll(x, 8, axis=1))


def _inproj_kernel(x_ref, sc_ref, sh_ref, g_ref, w_ref, cos_ref, sin_ref, o_ref, *, rope):
    x = x_ref[0]
    ms = jnp.mean(x * x, axis=-1, keepdims=True)
    y = x * lax.rsqrt(ms + EPS) * g_ref[...]
    h = (y * (1.0 + sc_ref[0]) + sh_ref[0]).astype(BF16)
    for j in range(PROJ_WIDTH // GROUP_WIDTH):
        cols = slice(j * GROUP_WIDTH, (j + 1) * GROUP_WIDTH)
        acc = jnp.dot(h, w_ref[:, cols], preferred_element_type=F32)
        if rope and j in (COL_AQ, COL_AK):
            acc = acc * cos_ref[...] + _swap8(acc) * sin_ref[...]
        if j == COL_AQ:
            acc = acc * (DA_QK ** -0.5)
        if j == COL_NQ:
            acc = acc * (HEAD_DIM ** -0.5)
        o_ref[0, :, cols] = acc.astype(o_ref.dtype)


def _inproj_call(x, sc, sh, g, w_bf16, cos_tab, sin_tab, *, rope, tm):
    B, L, D = x.shape
    per_batch = sc.shape[0] > 1
    mod_map = (lambda b, i: (b, 0, 0)) if per_batch else (lambda b, i: (0, 0, 0))
    tab_map = (lambda b, i: (i, 0)) if rope else (lambda b, i: (0, 0))
    return pl.pallas_call(
        functools.partial(_inproj_kernel, rope=rope),
        grid=(B, L // tm),
        in_specs=[pl.BlockSpec((1, tm, D), lambda b, i: (b, i, 0)),
                  pl.BlockSpec((1, 1, D), mod_map),
                  pl.BlockSpec((1, 1, D), mod_map),
                  pl.BlockSpec((1, D), lambda b, i: (0, 0)),
                  pl.BlockSpec((D, PROJ_WIDTH), lambda b, i: (0, 0)),
                  pl.BlockSpec((tm, GROUP_WIDTH), tab_map),
                  pl.BlockSpec((tm, GROUP_WIDTH), tab_map)],
        out_specs=pl.BlockSpec((1, tm, PROJ_WIDTH), lambda b, i: (b, i, 0)),
        out_shape=jax.ShapeDtypeStruct((B, L, PROJ_WIDTH), BF16),
        compiler_params=_cparams(("arbitrary", "arbitrary")),
        name="in_proj",
    )(x, sc, sh, g.reshape(1, D), w_bf16, cos_tab, sin_tab)


def _rope_tables(S):
    n = 8
    inv = ROPE_BASE ** (-jnp.arange(n, dtype=F32) / n)
    t = jnp.arange(S)
    row_pos = (t // GRID_W).astype(F32)
    col_pos = (t % GRID_W).astype(F32)
    parts_c, parts_s = [], []
    for pos in (row_pos, col_pos):
        ang = pos[:, None] * inv[None, :]
        c, s = jnp.cos(ang), jnp.sin(ang)
        parts_c += [c, c]
        parts_s += [-s, s]
    cos32 = jnp.concatenate(parts_c, axis=1)
    sin32 = jnp.concatenate(parts_s, axis=1)
    reps = GROUP_WIDTH // 32
    return jnp.tile(cos32, (1, reps)), jnp.tile(sin32, (1, reps))


def _da_kernel(lam_ref, *refs, n_lat, tkc, post_scale):
    if n_lat:
        q_ref, kl_ref, vl_ref, kc_ref, vc_ref, gsub_ref, gmat_ref, o_ref = refs
    else:
        q_ref, kc_ref, vc_ref, gsub_ref, gmat_ref, o_ref = refs
    q = q_ref[0]
    tq = q.shape[0]
    lane = _lane_iota(GROUP_WIDTH)
    lam = lam_ref[0]
    out = jnp.zeros((tq, GROUP_WIDTH), F32)
    for h in range(N_HEADS):
        res = []
        for mp in range(2):
            lo = h * HEAD_DIM + mp * DA_QK
            qm = q * ((lane >= lo) & (lane < lo + DA_QK)).astype(BF16)

            def step(k, v, carry, qm=qm):
                m, l, acc = carry
                s = _dot_nt(qm, k)
                m_new = jnp.maximum(m, jnp.max(s, axis=-1, keepdims=True))
                alpha = jnp.exp(m - m_new)
                p = jnp.exp(s - m_new)
                l = alpha * l + jnp.sum(p, axis=-1, keepdims=True)
                acc = alpha * acc + jnp.dot(p.astype(BF16), v, preferred_element_type=F32)
                return m_new, l, acc

            carry = (jnp.full((tq, 1), NEG_INF, F32), jnp.zeros((tq, 1), F32),
                     jnp.zeros((tq, GROUP_WIDTH), F32))
            if n_lat:
                def body(c, carry, step=step):
                    off = pl.multiple_of(c * tkc, tkc)
                    return step(kl_ref[0, pl.ds(off, tkc), :], vl_ref[0, pl.ds(off, tkc), :], carry)
                carry = lax.fori_loop(0, n_lat, body, carry)
            m, l, acc = step(kc_ref[0], vc_ref[0], carry)
            res.append(acc / l)
        o_h = res[0] - lam * res[1]
        hm = (lane >= h * HEAD_DIM) & (lane < (h + 1) * HEAD_DIM)
        out = jnp.where(hm, o_h, out)
    ms = jnp.dot(out * out, gmat_ref[...], preferred_element_type=F32,
                 precision=lax.Precision.HIGHEST)
    y = out * lax.rsqrt(ms + EPS) * gsub_ref[...] * post_scale
    o_ref[0] = y.astype(o_ref.dtype)


def _da_call(lam, proj_q, proj_lat, proj_ctx, gsub256, gmat, *, lam_init, tq, tkc):
    B, Lq, _ = proj_q.shape
    C = proj_ctx.shape[1]
    in_specs = [pl.BlockSpec(memory_space=pltpu.SMEM),
                pl.BlockSpec((1, tq, GROUP_WIDTH), lambda b, i: (b, i, COL_AQ))]
    args = [lam, proj_q]
    n_lat = 0
    if proj_lat is not None:
        S = proj_lat.shape[1]
        n_lat = S // tkc
        in_specs += [pl.BlockSpec((1, S, GROUP_WIDTH), lambda b, i: (b, 0, COL_AK)),
                     pl.BlockSpec((1, S, GROUP_WIDTH), lambda b, i: (b, 0, COL_AV))]
        args += [proj_lat, proj_lat]
    in_specs += [pl.BlockSpec((1, C, GROUP_WIDTH), lambda b, i: (b, 0, COL_AK)),
                 pl.BlockSpec((1, C, GROUP_WIDTH), lambda b, i: (b, 0, COL_AV)),
                 pl.BlockSpec((1, GROUP_WIDTH), lambda b, i: (0, 0)),
                 pl.BlockSpec((GROUP_WIDTH, GROUP_WIDTH), lambda b, i: (0, 0))]
    args += [proj_ctx, proj_ctx, gsub256, gmat]
    return pl.pallas_call(
        functools.partial(_da_kernel, n_lat=n_lat, tkc=tkc, post_scale=1.0 - lam_init),
        grid=(B, Lq // tq),
        in_specs=in_specs,
        out_specs=pl.BlockSpec((1, tq, GROUP_WIDTH), lambda b, i: (b, i, 0)),
        out_shape=jax.ShapeDtypeStruct((B, Lq, GROUP_WIDTH), BF16),
        compiler_params=_cparams(("arbitrary", "arbitrary")),
        name="diff_attn" if n_lat else "diff_attn_ctx",
    )(*args)


def _softmax_heads(q, sources, out_dtype):
    tq = q.shape[0]
    lane = _lane_iota(GROUP_WIDTH)
    out = jnp.zeros((tq, GROUP_WIDTH), F32)
    for h in range(N_HEADS):
        hm = (lane >= h * HEAD_DIM) & (lane < (h + 1) * HEAD_DIM)
        qm = q * hm.astype(BF16)
        scores = []
        for k, _, bias_fn in sources:
            s = _dot_nt(qm, k)
            if bias_fn is not None:
                s = s + bias_fn(h)
            scores.append(s)
        m = scores[0].max(axis=-1, keepdims=True)
        for s in scores[1:]:
            m = jnp.maximum(m, s.max(axis=-1, keepdims=True))
        l = jnp.zeros((tq, 1), F32)
        o = jnp.zeros((tq, GROUP_WIDTH), F32)
        for s, (_, v, _) in zip(scores, sources):
            p = jnp.exp(s - m)
            l = l + jnp.sum(p, axis=-1, keepdims=True)
            o = o + jnp.dot(p.astype(BF16), v, preferred_element_type=F32)
        out = jnp.where(hm, o / l, out)
    return out.astype(out_dtype)


def _na_kernel(q_ref, kn_ref, vn_ref, kc_ref, vc_ref, bias_ref, o_ref, *, n_groups):
    g = pl.program_id(1)
    tq = NA_QROWS * GRID_W
    band = NA_BAND * GRID_W
    start = pl.multiple_of(jnp.clip(g - 1, 0, n_groups - NA_BAND // NA_QROWS) * tq, tq)
    kb = kn_ref[0, pl.ds(start, band), :]
    vb = vn_ref[0, pl.ds(start, band), :]
    sources = [(kb, vb, lambda h: bias_ref[0, h]), (kc_ref[0], vc_ref[0], None)]
    o_ref[0] = _softmax_heads(q_ref[0], sources, o_ref.dtype)


def _na_call(proj, proj_ctx, bias):
    B, S, _ = proj.shape
    C = proj_ctx.shape[1]
    tq = NA_QROWS * GRID_W
    band = NA_BAND * GRID_W
    n_groups = S // tq

    def bias_map(b, g):
        return (jnp.where(g == 0, 0, jnp.where(g == n_groups - 1, 2, 1)), 0, 0, 0)

    return pl.pallas_call(
        functools.partial(_na_kernel, n_groups=n_groups),
        grid=(B, n_groups),
        in_specs=[pl.BlockSpec((1, tq, GROUP_WIDTH), lambda b, g: (b, g, COL_NQ)),
                  pl.BlockSpec((1, S, GROUP_WIDTH), lambda b, g: (b, 0, COL_NK)),
                  pl.BlockSpec((1, S, GROUP_WIDTH), lambda b, g: (b, 0, COL_NV)),
                  pl.BlockSpec((1, C, GROUP_WIDTH), lambda b, g: (b, 0, COL_NK)),
                  pl.BlockSpec((1, C, GROUP_WIDTH), lambda b, g: (b, 0, COL_NV)),
                  pl.BlockSpec((1, N_HEADS, tq, band), bias_map)],
        out_specs=pl.BlockSpec((1, tq, GROUP_WIDTH), lambda b, g: (b, g, 0)),
        out_shape=jax.ShapeDtypeStruct((B, S, GROUP_WIDTH), BF16),
        compiler_params=_cparams(("arbitrary", "arbitrary")),
        name="nbr_attn",
    )(proj, proj, proj, proj_ctx, proj_ctx, bias)


def _na_bias_tables(rpb, rows):
    tabs = []
    n_groups = rows // NA_QROWS
    for g in (0, 1, n_groups - 1):
        r0 = g * NA_QROWS
        start = min(max(r0 - WIN_R // 2, 0), rows - NA_BAND)
        r = r0 + np.arange(NA_QROWS)[:, None, None, None]
        c = np.arange(GRID_W)[None, :, None, None]
        kr = start + np.arange(NA_BAND)[None, None, :, None]
        kc = np.arange(GRID_W)[None, None, None, :]
        row_lo = np.clip(r - WIN_R // 2, 0, rows - WIN_R)
        col_lo = np.clip(c - WIN_C // 2, 0, GRID_W - WIN_C)
        ok = (kr >= row_lo) & (kr < row_lo + WIN_R) & (kc >= col_lo) & (kc < col_lo + WIN_C)
        rel_r = np.clip(kr - r + (WIN_R - 1), 0, 2 * WIN_R - 2)
        rel_c = np.clip(kc - c, -(WIN_C - 1), WIN_C - 1) + (WIN_C - 1)
        shape = (NA_QROWS, GRID_W, NA_BAND, GRID_W)
        rel_r = np.broadcast_to(rel_r, shape).reshape(NA_QROWS * GRID_W, NA_BAND * GRID_W)
        rel_c = np.broadcast_to(rel_c, shape).reshape(NA_QROWS * GRID_W, NA_BAND * GRID_W)
        ok = np.broadcast_to(ok, shape).reshape(NA_QROWS * GRID_W, NA_BAND * GRID_W)
        vals = rpb[:, rel_r, rel_c].astype(F32)
        tabs.append(jnp.where(ok[None], vals, NEG_INF))
    return jnp.stack(tabs, axis=0)


def _ctx_attn_kernel(q_ref, k_ref, v_ref, o_ref):
    o_ref[0] = _softmax_heads(q_ref[0], [(k_ref[0], v_ref[0], None)], o_ref.dtype)


def _ctx_attn_call(proj_ctx):
    B, C, _ = proj_ctx.shape
    return pl.pallas_call(
        _ctx_attn_kernel,
        grid=(B,),
        in_specs=[pl.BlockSpec((1, C, GROUP_WIDTH), lambda b: (b, 0, COL_NQ)),
                  pl.BlockSpec((1, C, GROUP_WIDTH), lambda b: (b, 0, COL_NK)),
                  pl.BlockSpec((1, C, GROUP_WIDTH), lambda b: (b, 0, COL_NV))],
        out_specs=pl.BlockSpec((1, C, GROUP_WIDTH), lambda b: (b, 0, 0)),
        out_shape=jax.ShapeDtypeStruct((B, C, GROUP_WIDTH), BF16),
        compiler_params=_cparams(("arbitrary",)),
        name="ctx_attn",
    )(proj_ctx, proj_ctx, proj_ctx)


def _gelu_tanh(x):
    return 0.5 * x * (1.0 + jnp.tanh(math.sqrt(2.0 / math.pi) * (x + 0.044715 * (x * x * x))))


def _mixers_kernel(pc_ref, pp_ref, pn_ref, sg_ref, wpool_ref, spool_ref, gsgu_ref, wsgu_ref,
                   bsgu_ref, pool_o, sgu_o, *, tm, seq_len):
    i = pl.program_id(1)
    n_tiles = seq_len // tm
    lane = _lane_iota(GROUP_WIDTH)
    grp = lax.shift_right_logical(lane, 6)

    def by_group(vals):
        return jnp.where(grp == 0, vals[0], jnp.where(grp == 1, vals[1],
                                                       jnp.where(grp == 2, vals[2], vals[3])))

    cur = pc_ref[0].astype(F32)
    prev = jnp.where(i > 0, pp_ref[0].astype(F32), 0.0)
    nxt = jnp.where(i < n_tiles - 1, pn_ref[0].astype(F32), 0.0)
    ext = jnp.concatenate([prev, cur, nxt], axis=0)
    n = tm + 2 * POOL_HALO

    def ahead(x, d):
        return pltpu.roll(x, n - d, axis=0)

    def behind(x, d):
        return pltpu.roll(x, d, axis=0)

    f2 = ext + ahead(ext, 1)
    f4 = f2 + ahead(f2, 2)
    f8 = f4 + ahead(f4, 4)
    f16 = f8 + ahead(f8, 8)
    sums = by_group([behind(f2, 1), behind(f4, 2), behind(f8, 4), behind(f16, 8)])
    sums = sums[POOL_HALO:POOL_HALO + tm]
    t = i * tm + lax.broadcasted_iota(jnp.int32, (tm, 1), 0)
    cnts = []
    for w in POOL_WINDOWS:
        lo = jnp.maximum(t - w // 2, 0)
        hi = jnp.minimum(t + (w - w // 2 - 1), seq_len - 1)
        cnts.append((hi - lo + 1).astype(F32))
    cnt = by_group(cnts)
    diff = (sums / cnt - cur).astype(BF16)
    pool = jnp.dot(diff, wpool_ref[...], preferred_element_type=F32) * spool_ref[...]
    pool_o[0] = pool.astype(pool_o.dtype)

    z = _gelu_tanh(sg_ref[0].astype(F32))
    u = z[:, :GROUP_WIDTH]
    v = z[:, GROUP_WIDTH:]
    v = v * lax.rsqrt(jnp.mean(v * v, axis=-1, keepdims=True) + EPS) * gsgu_ref[...]
    v = v.astype(BF16)
    hmasks = [(grp == h).astype(BF16) for h in range(N_HEADS)]
    for c in range(tm // SGU_CHUNK):
        rows = slice(c * SGU_CHUNK, (c + 1) * SGU_CHUNK)
        vc = v[rows]
        vstack = jnp.concatenate([vc * hm for hm in hmasks], axis=0)
        s = jnp.dot(wsgu_ref[...], vstack, preferred_element_type=F32) + bsgu_ref[...]
        sgu_o[0, rows, :] = (u[rows] * s).astype(sgu_o.dtype)


def _mixers_call(proj, wpool_bd, spool, gsgu, wsgu_cat, bsgu_full, *, tm):
    B, L, _ = proj.shape
    hb = tm // POOL_HALO
    n_halo = L // POOL_HALO
    return pl.pallas_call(
        functools.partial(_mixers_kernel, tm=tm, seq_len=L),
        grid=(B, L // tm),
        in_specs=[pl.BlockSpec((1, tm, GROUP_WIDTH), lambda b, i: (b, i, COL_POOL)),
                  pl.BlockSpec((1, POOL_HALO, GROUP_WIDTH),
                               lambda b, i: (b, jnp.maximum(i * hb - 1, 0), COL_POOL)),
                  pl.BlockSpec((1, POOL_HALO, GROUP_WIDTH),
                               lambda b, i: (b, jnp.minimum((i + 1) * hb, n_halo - 1), COL_POOL)),
                  pl.BlockSpec((1, tm, 2 * GROUP_WIDTH), lambda b, i: (b, i, COL_SGU // 2)),
                  pl.BlockSpec((GROUP_WIDTH, GROUP_WIDTH), lambda b, i: (0, 0)),
                  pl.BlockSpec((1, GROUP_WIDTH), lambda b, i: (0, 0)),
                  pl.BlockSpec((1, GROUP_WIDTH), lambda b, i: (0, 0)),
                  pl.BlockSpec((SGU_CHUNK, N_HEADS * SGU_CHUNK), lambda b, i: (0, 0)),
                  pl.BlockSpec((SGU_CHUNK, GROUP_WIDTH), lambda b, i: (0, 0))],
        out_specs=[pl.BlockSpec((1, tm, GROUP_WIDTH), lambda b, i: (b, i, 0)),
                   pl.BlockSpec((1, tm, GROUP_WIDTH), lambda b, i: (b, i, 0))],
        out_shape=[jax.ShapeDtypeStruct((B, L, GROUP_WIDTH), BF16),
                   jax.ShapeDtypeStruct((B, L, GROUP_WIDTH), BF16)],
        compiler_params=_cparams(("arbitrary", "arbitrary")),
        name="pool_sgu",
    )(proj, proj, proj, proj, wpool_bd, spool, gsgu, wsgu_cat, bsgu_full)


def _outproj_kernel(cnt0_ref, p0, p1, p2, p3, w_ref, x_ref, gt_ref, g_ref, sc_ref, sh_ref,
                    wr_ref, br_ref, xo_ref, hf_ref, idx_ref, gate_ref, rank_ref, cnt_ref, cnt_scr):
    first = (pl.program_id(0) == 0) & (pl.program_id(1) == 0)

    @pl.when(first)
    def _():
        cnt_scr[...] = cnt0_ref[...]

    acc = None
    for j, p in enumerate((p0, p1, p2, p3)):
        part = jnp.dot(p[0], w_ref[j * GROUP_WIDTH:(j + 1) * GROUP_WIDTH, :],
                       preferred_element_type=F32)
        acc = part if acc is None else acc + part
    x = x_ref[0] + gt_ref[0] * acc
    xo_ref[0] = x
    ms = jnp.mean(x * x, axis=-1, keepdims=True)
    hf = (x * lax.rsqrt(ms + EPS) * g_ref[...]) * (1.0 + sc_ref[0]) + sh_ref[0]
    hf_ref[0] = hf

    tm = x.shape[0]
    logits = jnp.dot(hf, wr_ref[...], preferred_element_type=F32,
                     precision=lax.Precision.HIGHEST) + br_ref[...]
    lane_e = lax.broadcasted_iota(jnp.int32, (tm, N_EXPERTS), 1).astype(F32)
    vals, idxs = [], []
    l = logits
    for _ in range(TOP_K):
        mx = jnp.max(l, axis=-1, keepdims=True)
        ik = jnp.min(jnp.where(l == mx, lane_e, float(N_EXPERTS)), axis=-1, keepdims=True)
        vals.append(mx)
        idxs.append(ik)
        l = jnp.where(lane_e == ik, -jnp.inf, l)
    exps = [jnp.exp(v - vals[0]) for v in vals]
    denom = exps[0] + exps[1] + exps[2] + exps[3]
    gates = [e / denom for e in exps]

    onehot = jnp.zeros((tm, N_EXPERTS), F32)
    for ik in idxs:
        onehot = onehot + (lane_e == ik).astype(F32)
    ri = lax.broadcasted_iota(jnp.int32, (tm, tm), 0)
    ci = lax.broadcasted_iota(jnp.int32, (tm, tm), 1)
    tri = (ri > ci).astype(BF16)
    before = jnp.dot(tri, onehot.astype(BF16), preferred_element_type=F32) + cnt_scr[...]
    ranks = [jnp.sum(jnp.where(lane_e == ik, before, 0.0), axis=-1, keepdims=True) for ik in idxs]
    cnt_scr[...] = cnt_scr[...] + jnp.sum(onehot, axis=0, keepdims=True)
    cnt_ref[...] = cnt_scr[...]

    lane = lax.broadcasted_iota(jnp.int32, (tm, 128), 1)

    def pack(cols):
        o = jnp.zeros((tm, 128), F32)
        for k, cvals in enumerate(cols):
            o = jnp.where(lane == k, cvals, o)
        return o

    idx_ref[0] = pack(idxs).astype(jnp.int32)
    gate_ref[0] = pack(gates)
    rank_ref[0] = pack(ranks).astype(jnp.int32)


def _outproj_call(cnt0, parts, w_out_bf16, x, gt, g, sc, sh, w_router, b_router, *, tm):
    B, L, D = x.shape
    per_batch = gt.shape[0] > 1
    mod_map = (lambda b, i: (b, 0, 0)) if per_batch else (lambda b, i: (0, 0, 0))
    part_spec = pl.BlockSpec((1, tm, GROUP_WIDTH), lambda b, i: (b, i, 0))
    row_spec = pl.BlockSpec((1, tm, D), lambda b, i: (b, i, 0))
    meta_spec = pl.BlockSpec((1, tm, 128), lambda b, i: (b, i, 0))
    cnt_spec = pl.BlockSpec((1, N_EXPERTS), lambda b, i: (0, 0))
    return pl.pallas_call(
        _outproj_kernel,
        grid=(B, L // tm),
        in_specs=[cnt_spec, part_spec, part_spec, part_spec, part_spec,
                  pl.BlockSpec((D, D), lambda b, i: (0, 0)),
                  row_spec,
                  pl.BlockSpec((1, 1, D), mod_map),
                  pl.BlockSpec((1, D), lambda b, i: (0, 0)),
                  pl.BlockSpec((1, 1, D), mod_map),
                  pl.BlockSpec((1, 1, D), mod_map),
                  pl.BlockSpec((D, N_EXPERTS), lambda b, i: (0, 0)),
                  cnt_spec],
        out_specs=[row_spec, row_spec, meta_spec, meta_spec, meta_spec, cnt_spec],
        out_shape=[jax.ShapeDtypeStruct((B, L, D), F32),
                   jax.ShapeDtypeStruct((B, L, D), F32),
                   jax.ShapeDtypeStruct((B, L, 128), jnp.int32),
                   jax.ShapeDtypeStruct((B, L, 128), F32),
                   jax.ShapeDtypeStruct((B, L, 128), jnp.int32),
                   jax.ShapeDtypeStruct((1, N_EXPERTS), F32)],
        scratch_shapes=[pltpu.VMEM((1, N_EXPERTS), F32)],
        compiler_params=_cparams(("arbitrary", "arbitrary")),
        name="out_proj_router",
    )(cnt0, *parts, w_out_bf16, x, gt, g.reshape(1, D), sc, sh, w_router,
      b_router.reshape(1, N_EXPERTS))


def _dispatch_kernel(dest_ref, hf_ref, xs_in, xs_out, sem, *, tm):
    del xs_in

    def body(i, carry):
        for k in range(TOP_K):
            d = dest_ref[0, 0, i * TOP_K + k]
            pltpu.make_async_copy(hf_ref.at[pl.ds(i, 1)], xs_out.at[pl.ds(d, 1)], sem).start()
        return carry

    lax.fori_loop(0, tm, body, 0)
    for k in range(TOP_K):
        pltpu.make_async_copy(hf_ref, xs_out.at[pl.ds(0, tm)], sem).wait()


def _dispatch_call(dest, hf2d, xs, *, tm):
    T, D = hf2d.shape
    n_tiles = T // tm
    return pl.pallas_call(
        functools.partial(_dispatch_kernel, tm=tm),
        grid=(n_tiles,),
        in_specs=[pl.BlockSpec((1, 1, tm * TOP_K), lambda i: (i, 0, 0), memory_space=pltpu.SMEM),
                  pl.BlockSpec((tm, D), lambda i: (i, 0)),
                  pl.BlockSpec(memory_space=pl.ANY)],
        out_specs=pl.BlockSpec(memory_space=pl.ANY),
        out_shape=jax.ShapeDtypeStruct(xs.shape, xs.dtype),
        scratch_shapes=[pltpu.SemaphoreType.DMA(())],
        input_output_aliases={2: 0},
        compiler_params=_cparams(("arbitrary",)),
        name="moe_dispatch",
    )(dest.reshape(n_tiles, 1, tm * TOP_K), hf2d, xs)


def _expert_kernel(be_ref, na_ref, xs_ref, wgu_ref, bgu_ref, wd_ref, bd_ref, y_ref):
    del be_ref

    @pl.when(pl.program_id(0) < na_ref[0])
    def _():
        xb = xs_ref[...].astype(BF16)
        gu = jnp.dot(xb, wgu_ref[0], preferred_element_type=F32) + bgu_ref[0]
        gt = jnp.minimum(gu[:, :D_EXPERT], SWIGLU_LIMIT)
        up = jnp.clip(gu[:, D_EXPERT:], -SWIGLU_LIMIT, SWIGLU_LIMIT)
        act = (up + 1.0) * gt * (1.0 / (1.0 + jnp.exp(-SWIGLU_ALPHA * gt)))
        y_ref[...] = jnp.dot(act.astype(BF16), wd_ref[0], preferred_element_type=F32) + bd_ref[0]

    @pl.when(pl.program_id(0) >= na_ref[0])
    def _():
        y_ref[...] = jnp.zeros(y_ref.shape, y_ref.dtype)


def _expert_call(blk_e, n_active, xs, wgu, bgu, wd, bd):
    n_slots, D = xs.shape
    nb = n_slots // MOE_BM

    def row_map(i, be, na):
        return (jnp.minimum(i, na[0] - 1), 0)

    def w_map(i, be, na):
        return (be[jnp.minimum(i, na[0] - 1)], 0, 0)

    return pl.pallas_call(
        _expert_kernel,
        grid_spec=pltpu.PrefetchScalarGridSpec(
            num_scalar_prefetch=2,
            grid=(nb,),
            in_specs=[pl.BlockSpec((MOE_BM, D), row_map),
                      pl.BlockSpec((1, D, 2 * D_EXPERT), w_map),
                      pl.BlockSpec((1, 1, 2 * D_EXPERT), w_map),
                      pl.BlockSpec((1, D_EXPERT, D), w_map),
                      pl.BlockSpec((1, 1, D), w_map)],
            out_specs=pl.BlockSpec((MOE_BM, D), lambda i, be, na: (i, 0))),
        out_shape=jax.ShapeDtypeStruct((n_slots, D), F32),
        compiler_params=_cparams(("arbitrary",)),
        name="moe_experts",
    )(blk_e, n_active, xs, wgu, bgu.reshape(N_EXPERTS, 1, 2 * D_EXPERT), wd,
      bd.reshape(N_EXPERTS, 1, D))


def _combine_kernel(dest_ref, gate_ref, x_ref, gt_ref, gfin_ref, yb_ref, o_ref, buf, sem, *,
                    tm, final_norm):
    def body(i, carry):
        for k in range(TOP_K):
            d = dest_ref[0, 0, i * TOP_K + k]
            pltpu.make_async_copy(yb_ref.at[pl.ds(d, 1)], buf.at[k, pl.ds(i, 1)], sem).start()
        return carry

    lax.fori_loop(0, tm, body, 0)
    for k in range(TOP_K):
        pltpu.make_async_copy(yb_ref.at[pl.ds(0, tm)], buf.at[k], sem).wait()
    gates = gate_ref[0]
    y = buf[0] * gates[:, 0:1]
    for k in range(1, TOP_K):
        y = y + buf[k] * gates[:, k:k + 1]
    x = x_ref[0] + gt_ref[0] * y
    if final_norm:
        ms = jnp.mean(x * x, axis=-1, keepdims=True)
        x = x * lax.rsqrt(ms + EPS) * gfin_ref[...]
    o_ref[0] = x


def _combine_call(dest, gates, x, gt, g_final, yb, *, tm, final_norm):
    B, L, D = x.shape
    n_l = L // tm
    per_batch = gt.shape[0] > 1
    mod_map = (lambda b, i: (b, 0, 0)) if per_batch else (lambda b, i: (0, 0, 0))
    return pl.pallas_call(
        functools.partial(_combine_kernel, tm=tm, final_norm=final_norm),
        grid=(B, n_l),
        in_specs=[pl.BlockSpec((1, 1, tm * TOP_K), lambda b, i: (b * n_l + i, 0, 0),
                               memory_space=pltpu.SMEM),
                  pl.BlockSpec((1, tm, 128), lambda b, i: (b, i, 0)),
                  pl.BlockSpec((1, tm, D), lambda b, i: (b, i, 0)),
                  pl.BlockSpec((1, 1, D), mod_map),
                  pl.BlockSpec((1, D), lambda b, i: (0, 0)),
                  pl.BlockSpec(memory_space=pl.ANY)],
        out_specs=pl.BlockSpec((1, tm, D), lambda b, i: (b, i, 0)),
        out_shape=jax.ShapeDtypeStruct((B, L, D), F32),
        scratch_shapes=[pltpu.VMEM((TOP_K, tm, D), F32), pltpu.SemaphoreType.DMA(())],
        compiler_params=_cparams(("arbitrary", "arbitrary")),
        name="moe_combine",
    )(dest.reshape(B * n_l, 1, tm * TOP_K), gates, x, gt, g_final.reshape(1, D), yb)


def _moe(streams, p_bf16, b_gu, b_down, g_final, final_norm):
    counts = streams[-1]["cnt"].reshape(N_EXPERTS).astype(jnp.int32)
    padded = (counts + MOE_BM - 1) // MOE_BM * MOE_BM
    cum_pad = jnp.cumsum(padded)
    pstart = cum_pad - padded
    total_tokens = sum(s["x"].shape[0] * s["x"].shape[1] for s in streams)
    nb = -(-total_tokens * TOP_K // MOE_BM) + N_EXPERTS
    blk_e = jnp.minimum(jnp.searchsorted(cum_pad, jnp.arange(nb) * MOE_BM, side="right"),
                        N_EXPERTS - 1).astype(jnp.int32)
    n_active = (cum_pad[-1] // MOE_BM).astype(jnp.int32).reshape(1)

    xs = jnp.zeros((nb * MOE_BM, D_MODEL), F32)
    dests = []
    for s in streams:
        B, L, _ = s["x"].shape
        idx = s["idx"][..., :TOP_K]
        dest = (pstart[idx] + s["rank"][..., :TOP_K]).reshape(B * L * TOP_K)
        dests.append(dest)
        xs = _dispatch_call(dest, s["hf"].reshape(B * L, D_MODEL), xs, tm=min(256, L))
    yb = _expert_call(blk_e, n_active, xs, p_bf16["w_gu"], b_gu, p_bf16["w_down"], b_down)
    outs = []
    for s, dest in zip(streams, dests):
        L = s["x"].shape[1]
        outs.append(_combine_call(dest, s["gate"], s["x"], s["gt"], g_final, yb,
                                  tm=min(128, L), final_norm=final_norm))
    return outs


def _layer(x, xc, c8, p, layer_idx, update_ctx, tabs, g_final, final_norm):
    B, S, D = x.shape
    C = xc.shape[1]
    lam_init = 0.8 - 0.6 * math.exp(-0.3 * layer_idx)
    lam = (jnp.exp(jnp.sum(p["lam_q1"] * p["lam_k1"])) - jnp.exp(jnp.sum(p["lam_q2"] * p["lam_k2"]))
           + lam_init).reshape(1).astype(F32)

    mod8 = _mod_call(c8, p["w_mod"], p["b_mod"])
    mod = [m.reshape(B, 1, D) for m in jnp.split(mod8[:B], 6, axis=-1)]
    modc = [m.reshape(1, 1, D) for m in jnp.split(mod8[B:B + 1], 6, axis=-1)]
    sh1, sc1, gt1, sh2, sc2, gt2 = mod

    w_in = p["w_in"].astype(BF16)
    w_out = p["w_out"].astype(BF16)
    proj = _inproj_call(x, sc1, sh1, p["g_mix"], w_in, tabs["cos"], tabs["sin"], rope=True, tm=512)
    projc = _inproj_call(xc, modc[1], modc[0], p["g_mix"], w_in, tabs["cos"], tabs["sin"],
                         rope=False, tm=C)

    gsub256 = jnp.tile(p["g_sub"], N_HEADS).reshape(1, GROUP_WIDTH)
    oa = _da_call(lam, proj, proj, projc, gsub256, tabs["gmat"], lam_init=lam_init, tq=256, tkc=512)
    bias = _na_bias_tables(p["rpb"], S // GRID_W)
    od = _na_call(proj, projc, bias)

    wpool_bd = jax.scipy.linalg.block_diag(*[p["w_pool"][g] for g in range(4)]).astype(BF16)
    spool = p["s_pool"].reshape(1, GROUP_WIDTH)
    gsgu = p["g_sgu"].reshape(1, GROUP_WIDTH)
    wsgu_cat = jnp.transpose(p["w_sgu"], (1, 0, 2)).reshape(SGU_CHUNK, N_HEADS * SGU_CHUNK).astype(BF16)
    bsgu_full = jnp.repeat(p["b_sgu"].T, HEAD_DIM, axis=1)
    pool, sgu = _mixers_call(proj, wpool_bd, spool, gsgu, wsgu_cat, bsgu_full, tm=512)

    cnt0 = jnp.zeros((1, N_EXPERTS), F32)
    x1, hf, idx, gate, rank, cnt = _outproj_call(
        cnt0, (oa, pool, sgu, od), w_out, x, gt1, p["g_ffn"], sc2, sh2,
        p["w_router"], p["b_router"], tm=256)
    streams = [dict(x=x1, hf=hf, idx=idx, gate=gate, rank=rank, gt=gt2, cnt=cnt)]

    if update_ctx:
        oa_c = _da_call(lam, projc, None, projc, gsub256, tabs["gmat"], lam_init=lam_init,
                        tq=C, tkc=512)
        od_c = _ctx_attn_call(projc)
        pool_c, sgu_c = _mixers_call(projc, wpool_bd, spool, gsgu, wsgu_cat, bsgu_full, tm=C)
        xc1, hfc, idxc, gatec, rankc, cnt = _outproj_call(
            cnt, (oa_c, pool_c, sgu_c, od_c), w_out, xc, modc[2], p["g_ffn"], modc[4], modc[3],
            p["w_router"], p["b_router"], tm=C)
        streams.append(dict(x=xc1, hf=hfc, idx=idxc, gate=gatec, rank=rankc, gt=modc[5], cnt=cnt))

    p_bf16 = dict(w_gu=p["w_gu"].astype(BF16), w_down=p["w_down"].astype(BF16))
    outs = _moe(streams, p_bf16, p["b_gu"], p["b_down"], g_final, final_norm)
    if update_ctx:
        return outs[0], outs[1]
    return outs[0], xc


def kernel(x, c, ctx, c_ctx, w_mod, b_mod, g_mix, g_ffn, w_in, w_out, lam_q1, lam_k1, lam_q2,
           lam_k2, g_sub, w_pool, s_pool, g_sgu, w_sgu, b_sgu, rpb, w_router, b_router, w_gu,
           b_gu, w_down, b_down, g_final):
    B, S, D = x.shape
    cos_tab, sin_tab = _rope_tables(S)
    gidx = np.arange(GROUP_WIDTH) // HEAD_DIM
    gmat = jnp.asarray((gidx[:, None] == gidx[None, :]).astype(np.float32) / HEAD_DIM)
    tabs = dict(cos=cos_tab, sin=sin_tab, gmat=gmat)
    c8 = jnp.concatenate([c, c_ctx[None, :], jnp.zeros((8 - B - 1, D), F32)], axis=0)
    xc = ctx
    for l in range(DEPTH):
        p = dict(w_mod=w_mod[l], b_mod=b_mod[l], g_mix=g_mix[l], g_ffn=g_ffn[l], w_in=w_in[l],
                 w_out=w_out[l], lam_q1=lam_q1[l], lam_k1=lam_k1[l], lam_q2=lam_q2[l],
                 lam_k2=lam_k2[l], g_sub=g_sub[l], w_pool=w_pool[l], s_pool=s_pool[l],
                 g_sgu=g_sgu[l], w_sgu=w_sgu[l], b_sgu=b_sgu[l], rpb=rpb[l],
                 w_router=w_router[l], b_router=b_router[l], w_gu=w_gu[l], b_gu=b_gu[l],
                 w_down=w_down[l], b_down=b_down[l])
        x, xc = _layer(x, xc, c8, p, l, l < DEPTH - 1, tabs, g_final, l == DEPTH - 1)
    return x
```

```python
import functools
import math

import jax
import jax.numpy as jnp
import numpy as np
from jax import lax
from jax.experimental import pallas as pl
from jax.experimental.pallas import tpu as pltpu

F32 = jnp.float32
BF16 = jnp.bfloat16

D_MODEL = 1024
DEPTH = 2
GRID_W = 64
HEAD_DIM = 64
N_HEADS = 4
GROUP_WIDTH = N_HEADS * HEAD_DIM
PROJ_WIDTH = 9 * GROUP_WIDTH
DA_QK = HEAD_DIM // 2
ROPE_BASE = 10000.0
POOL_WINDOWS = (2, 4, 8, 16)
SGU_CHUNK = 128
WIN_R = 8
WIN_C = 16
N_EXPERTS = 32
TOP_K = 4
D_EXPERT = D_MODEL
SWIGLU_LIMIT = 7.0
SWIGLU_ALPHA = 1.702
EPS = 1e-6
NEG_INF = -1e30

COL_AQ, COL_AK, COL_AV, COL_POOL, COL_SGU, COL_NQ, COL_NK, COL_NV = 0, 1, 2, 3, 4, 6, 7, 8

NA_QROWS = 4
NA_BAND = 12
POOL_HALO = 16
MOE_BM = 256
VMEM_LIMIT = 48 * 1024 * 1024


def _cparams(sem):
    return pltpu.CompilerParams(dimension_semantics=sem, vmem_limit_bytes=VMEM_LIMIT)


def _lane_iota(n):
    return lax.broadcasted_iota(jnp.int32, (1, n), 1)


def _dot_nt(a, b):
    return lax.dot_general(a, b, (((1,), (1,)), ((), ())), preferred_element_type=F32)


def _mod_kernel(c_ref, w_ref, b_ref, o_ref):
    c = c_ref[...]
    s = c * (1.0 / (1.0 + jnp.exp(-c)))
    o_ref[...] = jnp.dot(s.astype(BF16), w_ref[...].astype(BF16),
                         preferred_element_type=F32) + b_ref[...]


def _mod_call(c8, w_mod, b_mod):
    n = w_mod.shape[1]
    tn = 1536
    return pl.pallas_call(
        _mod_kernel,
        grid=(n // tn,),
        in_specs=[pl.BlockSpec((8, D_MODEL), lambda j: (0, 0)),
                  pl.BlockSpec((D_MODEL, tn), lambda j: (0, j)),
                  pl.BlockSpec((1, tn), lambda j: (0, j))],
        out_specs=pl.BlockSpec((8, tn), lambda j: (0, j)),
        out_shape=jax.ShapeDtypeStruct((8, n), F32),
        compiler_params=_cparams(("arbitrary",)),
        name="adaln_mod",
    )(c8, w_mod, b_mod.reshape(1, n))


def _swap8(x):
    n = x.shape[-1]
    lane = _lane_iota(n)
    first_half = (lane & 15) < 8
    return jnp.where(first_half, pltpu.roll(x, n - 8, axis=1), pltpu.roll(x, 8, axis=1))


def _inproj_kernel(x_ref, sc_ref, sh_ref, g_ref, w_ref, cos_ref, sin_ref, o_ref, *, rope):
    x = x_ref[0]
    ms = jnp.mean(x * x, axis=-1, keepdims=True)
    y = x * lax.rsqrt(ms + EPS) * g_ref[...]
    h = (y * (1.0 + sc_ref[0]) + sh_ref[0]).astype(BF16)
    for j in range(PROJ_WIDTH // GROUP_WIDTH):
        cols = slice(j * GROUP_WIDTH, (j + 1) * GROUP_WIDTH)
        acc = jnp.dot(h, w_ref[:, cols], preferred_element_type=F32)
        if rope and j in (COL_AQ, COL_AK):
            acc = acc * cos_ref[...] + _swap8(acc) * sin_ref[...]
        if j == COL_AQ:
            acc = acc * (DA_QK ** -0.5)
        if j == COL_NQ:
            acc = acc * (HEAD_DIM ** -0.5)
        o_ref[0, :, cols] = acc.astype(o_ref.dtype)


def _inproj_call(x, sc, sh, g, w_bf16, cos_tab, sin_tab, *, rope, tm):
    B, L, D = x.shape
    per_batch = sc.shape[0] > 1
    mod_map = (lambda b, i: (b, 0, 0)) if per_batch else (lambda b, i: (0, 0, 0))
    tab_map = (lambda b, i: (i, 0)) if rope else (lambda b, i: (0, 0))
    return pl.pallas_call(
        functools.partial(_inproj_kernel, rope=rope),
        grid=(B, L // tm),
        in_specs=[pl.BlockSpec((1, tm, D), lambda b, i: (b, i, 0)),
                  pl.BlockSpec((1, 1, D), mod_map),
                  pl.BlockSpec((1, 1, D), mod_map),
                  pl.BlockSpec((1, D), lambda b, i: (0, 0)),
                  pl.BlockSpec((D, PROJ_WIDTH), lambda b, i: (0, 0)),
                  pl.BlockSpec((tm, GROUP_WIDTH), tab_map),
                  pl.BlockSpec((tm, GROUP_WIDTH), tab_map)],
        out_specs=pl.BlockSpec((1, tm, PROJ_WIDTH), lambda b, i: (b, i, 0)),
        out_shape=jax.ShapeDtypeStruct((B, L, PROJ_WIDTH), BF16),
        compiler_params=_cparams(("arbitrary", "arbitrary")),
        name="in_proj",
    )(x, sc, sh, g.reshape(1, D), w_bf16, cos_tab, sin_tab)


def _rope_tables(S):
    n = 8
    inv = ROPE_BASE ** (-jnp.arange(n, dtype=F32) / n)
    t = jnp.arange(S)
    row_pos = (t // GRID_W).astype(F32)
    col_pos = (t % GRID_W).astype(F32)
    parts_c, parts_s = [], []
    for pos in (row_pos, col_pos):
        ang = pos[:, None] * inv[None, :]
        c, s = jnp.cos(ang), jnp.sin(ang)
        parts_c += [c, c]
        parts_s += [-s, s]
    cos32 = jnp.concatenate(parts_c, axis=1)
    sin32 = jnp.concatenate(parts_s, axis=1)
    reps = GROUP_WIDTH // 32
    return jnp.tile(cos32, (1, reps)), jnp.tile(sin32, (1, reps))


def _da_kernel(lam_ref, *refs, n_lat, tkc, post_scale):
    if n_lat:
        q_ref, kl_ref, vl_ref, kc_ref, vc_ref, gsub_ref, gmat_ref, o_ref = refs
    else:
        q_ref, kc_ref, vc_ref, gsub_ref, gmat_ref, o_ref = refs
    q = q_ref[0]
    tq = q.shape[0]
    lane = _lane_iota(GROUP_WIDTH)
    lam = lam_ref[0]
    out = jnp.zeros((tq, GROUP_WIDTH), F32)
    for h in range(N_HEADS):
        res = []
        for mp in range(2):
            lo = h * HEAD_DIM + mp * DA_QK
            qm = q * ((lane >= lo) & (lane < lo + DA_QK)).astype(BF16)

            def step(k, v, carry, qm=qm):
                m, l, acc = carry
                s = _dot_nt(qm, k)
                m_new = jnp.maximum(m, jnp.max(s, axis=-1, keepdims=True))
                alpha = jnp.exp(m - m_new)
                p = jnp.exp(s - m_new)
                l = alpha * l + jnp.sum(p, axis=-1, keepdims=True)
                acc = alpha * acc + jnp.dot(p.astype(BF16), v, preferred_element_type=F32)
                return m_new, l, acc

            carry = (jnp.full((tq, 1), NEG_INF, F32), jnp.zeros((tq, 1), F32),
                     jnp.zeros((tq, GROUP_WIDTH), F32))
            if n_lat:
                def body(c, carry, step=step):
                    off = pl.multiple_of(c * tkc, tkc)
                    return step(kl_ref[0, pl.ds(off, tkc), :], vl_ref[0, pl.ds(off, tkc), :], carry)
                carry = lax.fori_loop(0, n_lat, body, carry)
            m, l, acc = step(kc_ref[0], vc_ref[0], carry)
            res.append(acc / l)
        o_h = res[0] - lam * res[1]
        hm = (lane >= h * HEAD_DIM) & (lane < (h + 1) * HEAD_DIM)
        out = jnp.where(hm, o_h, out)
    ms = jnp.dot(out * out, gmat_ref[...], preferred_element_type=F32,
                 precision=lax.Precision.HIGHEST)
    y = out * lax.rsqrt(ms + EPS) * gsub_ref[...] * post_scale
    o_ref[0] = y.astype(o_ref.dtype)


def _da_call(lam, proj_q, proj_lat, proj_ctx, gsub256, gmat, *, lam_init, tq, tkc):
    B, Lq, _ = proj_q.shape
    C = proj_ctx.shape[1]
    in_specs = [pl.BlockSpec(memory_space=pltpu.SMEM),
                pl.BlockSpec((1, tq, GROUP_WIDTH), lambda b, i: (b, i, COL_AQ))]
    args = [lam, proj_q]
    n_lat = 0
    if proj_lat is not None:
        S = proj_lat.shape[1]
        n_lat = S // tkc
        in_specs += [pl.BlockSpec((1, S, GROUP_WIDTH), lambda b, i: (b, 0, COL_AK)),
                     pl.BlockSpec((1, S, GROUP_WIDTH), lambda b, i: (b, 0, COL_AV))]
        args += [proj_lat, proj_lat]
    in_specs += [pl.BlockSpec((1, C, GROUP_WIDTH), lambda b, i: (b, 0, COL_AK)),
                 pl.BlockSpec((1, C, GROUP_WIDTH), lambda b, i: (b, 0, COL_AV)),
                 pl.BlockSpec((1, GROUP_WIDTH), lambda b, i: (0, 0)),
                 pl.BlockSpec((GROUP_WIDTH, GROUP_WIDTH), lambda b, i: (0, 0))]
    args += [proj_ctx, proj_ctx, gsub256, gmat]
    return pl.pallas_call(
        functools.partial(_da_kernel, n_lat=n_lat, tkc=tkc, post_scale=1.0 - lam_init),
        grid=(B, Lq // tq),
        in_specs=in_specs,
        out_specs=pl.BlockSpec((1, tq, GROUP_WIDTH), lambda b, i: (b, i, 0)),
        out_shape=jax.ShapeDtypeStruct((B, Lq, GROUP_WIDTH), BF16),
        compiler_params=_cparams(("arbitrary", "arbitrary")),
        name="diff_attn" if n_lat else "diff_attn_ctx",
    )(*args)


def _softmax_heads(q, sources, out_dtype):
    tq = q.shape[0]
    lane = _lane_iota(GROUP_WIDTH)
    out = jnp.zeros((tq, GROUP_WIDTH), F32)
    for h in range(N_HEADS):
        hm = (lane >= h * HEAD_DIM) & (lane < (h + 1) * HEAD_DIM)
        qm = q * hm.astype(BF16)
        scores = []
        for k, _, bias_fn in sources:
            s = _dot_nt(qm, k)
            if bias_fn is not None:
                s = s + bias_fn(h)
            scores.append(s)
        m = scores[0].max(axis=-1, keepdims=True)
        for s in scores[1:]:
            m = jnp.maximum(m, s.max(axis=-1, keepdims=True))
        l = jnp.zeros((tq, 1), F32)
        o = jnp.zeros((tq, GROUP_WIDTH), F32)
        for s, (_, v, _) in zip(scores, sources):
            p = jnp.exp(s - m)
            l = l + jnp.sum(p, axis=-1, keepdims=True)
            o = o + jnp.dot(p.astype(BF16), v, preferred_element_type=F32)
        out = jnp.where(hm, o / l, out)
    return out.astype(out_dtype)


def _na_kernel(q_ref, kn_ref, vn_ref, kc_ref, vc_ref, bias_ref, o_ref, *, n_groups):
    g = pl.program_id(1)
    tq = NA_QROWS * GRID_W
    band = NA_BAND * GRID_W
    start = pl.multiple_of(jnp.clip(g - 1, 0, n_groups - NA_BAND // NA_QROWS) * tq, tq)
    kb = kn_ref[0, pl.ds(start, band), :]
    vb = vn_ref[0, pl.ds(start, band), :]
    sources = [(kb, vb, lambda h: bias_ref[0, h]), (kc_ref[0], vc_ref[0], None)]
    o_ref[0] = _softmax_heads(q_ref[0], sources, o_ref.dtype)


def _na_call(proj, proj_ctx, bias):
    B, S, _ = proj.shape
    C = proj_ctx.shape[1]
    tq = NA_QROWS * GRID_W
    band = NA_BAND * GRID_W
    n_groups = S // tq

    def bias_map(b, g):
        return (jnp.where(g == 0, 0, jnp.where(g == n_groups - 1, 2, 1)), 0, 0, 0)

    return pl.pallas_call(
        functools.partial(_na_kernel, n_groups=n_groups),
        grid=(B, n_groups),
        in_specs=[pl.BlockSpec((1, tq, GROUP_WIDTH), lambda b, g: (b, g, COL_NQ)),
                  pl.BlockSpec((1, S, GROUP_WIDTH), lambda b, g: (b, 0, COL_NK)),
                  pl.BlockSpec((1, S, GROUP_WIDTH), lambda b, g: (b, 0, COL_NV)),
                  pl.BlockSpec((1, C, GROUP_WIDTH), lambda b, g: (b, 0, COL_NK)),
                  pl.BlockSpec((1, C, GROUP_WIDTH), lambda b, g: (b, 0, COL_NV)),
                  pl.BlockSpec((1, N_HEADS, tq, band), bias_map)],
        out_specs=pl.BlockSpec((1, tq, GROUP_WIDTH), lambda b, g: (b, g, 0)),
        out_shape=jax.ShapeDtypeStruct((B, S, GROUP_WIDTH), BF16),
        compiler_params=_cparams(("arbitrary", "arbitrary")),
        name="nbr_attn",
    )(proj, proj, proj, proj_ctx, proj_ctx, bias)


def _na_bias_tables(rpb, rows):
    n_groups = rows // NA_QROWS
    n_rr, n_rc = 2 * WIN_R - 1, 2 * WIN_C - 1
    c = np.arange(GRID_W)[:, None]
    kc = np.arange(GRID_W)[None, :]
    col_lo = np.clip(c - WIN_C // 2, 0, GRID_W - WIN_C)
    col_ok = (kc >= col_lo) & (kc < col_lo + WIN_C)
    rel_c = np.clip(kc - c, -(WIN_C - 1), WIN_C - 1) + (WIN_C - 1)
    sel_c = (rel_c[..., None] == np.arange(n_rc)).astype(np.float32)
    sel_r, ok = [], []
    for g in (0, 1, n_groups - 1):
        r0 = g * NA_QROWS
        start = min(max(r0 - WIN_R // 2, 0), rows - NA_BAND)
        r = r0 + np.arange(NA_QROWS)[:, None]
        kr = start + np.arange(NA_BAND)[None, :]
        row_lo = np.clip(r - WIN_R // 2, 0, rows - WIN_R)
        row_ok = (kr >= row_lo) & (kr < row_lo + WIN_R)
        rel_r = kr - r + (WIN_R - 1)
        sel_r.append((rel_r[..., None] == np.arange(n_rr)).astype(np.float32))
        ok.append(row_ok[:, None, :, None] & col_ok[None, :, None, :])
    sel_r = jnp.asarray(np.stack(sel_r))
    ok = np.stack(ok).reshape(3, 1, NA_QROWS * GRID_W, NA_BAND * GRID_W)
    vals = jnp.einsum("vxka,hab,cyb->vhxcky", sel_r, rpb.astype(F32), jnp.asarray(sel_c),
                      precision=lax.Precision.HIGHEST)
    vals = vals.reshape(3, N_HEADS, NA_QROWS * GRID_W, NA_BAND * GRID_W)
    return jnp.where(jnp.asarray(ok), vals, NEG_INF)


def _ctx_attn_kernel(q_ref, k_ref, v_ref, o_ref):
    o_ref[0] = _softmax_heads(q_ref[0], [(k_ref[0], v_ref[0], None)], o_ref.dtype)


def _ctx_attn_call(proj_ctx):
    B, C, _ = proj_ctx.shape
    return pl.pallas_call(
        _ctx_attn_kernel,
        grid=(B,),
        in_specs=[pl.BlockSpec((1, C, GROUP_WIDTH), lambda b: (b, 0, COL_NQ)),
                  pl.BlockSpec((1, C, GROUP_WIDTH), lambda b: (b, 0, COL_NK)),
                  pl.BlockSpec((1, C, GROUP_WIDTH), lambda b: (b, 0, COL_NV))],
        out_specs=pl.BlockSpec((1, C, GROUP_WIDTH), lambda b: (b, 0, 0)),
        out_shape=jax.ShapeDtypeStruct((B, C, GROUP_WIDTH), BF16),
        compiler_params=_cparams(("arbitrary",)),
        name="ctx_attn",
    )(proj_ctx, proj_ctx, proj_ctx)


def _gelu_tanh(x):
    return 0.5 * x * (1.0 + jnp.tanh(math.sqrt(2.0 / math.pi) * (x + 0.044715 * (x * x * x))))


def _mixers_kernel(pc_ref, pp_ref, pn_ref, sg_ref, wpool_ref, spool_ref, gsgu_ref, wsgu_ref,
                   bsgu_ref, pool_o, sgu_o, *, tm, seq_len):
    i = pl.program_id(1)
    n_tiles = seq_len // tm
    lane = _lane_iota(GROUP_WIDTH)
    grp = lax.shift_right_logical(lane, 6)

    def by_group(vals):
        return jnp.where(grp == 0, vals[0], jnp.where(grp == 1, vals[1],
                                                       jnp.where(grp == 2, vals[2], vals[3])))

    cur = pc_ref[0].astype(F32)
    prev = jnp.where(i > 0, pp_ref[0].astype(F32), 0.0)
    nxt = jnp.where(i < n_tiles - 1, pn_ref[0].astype(F32), 0.0)
    ext = jnp.concatenate([prev, cur, nxt], axis=0)
    n = tm + 2 * POOL_HALO

    def ahead(x, d):
        return pltpu.roll(x, n - d, axis=0)

    def behind(x, d):
        return pltpu.roll(x, d, axis=0)

    f2 = ext + ahead(ext, 1)
    f4 = f2 + ahead(f2, 2)
    f8 = f4 + ahead(f4, 4)
    f16 = f8 + ahead(f8, 8)
    sums = by_group([behind(f2, 1), behind(f4, 2), behind(f8, 4), behind(f16, 8)])
    sums = sums[POOL_HALO:POOL_HALO + tm]
    t = i * tm + lax.broadcasted_iota(jnp.int32, (tm, 1), 0)
    cnts = []
    for w in POOL_WINDOWS:
        lo = jnp.maximum(t - w // 2, 0)
        hi = jnp.minimum(t + (w - w // 2 - 1), seq_len - 1)
        cnts.append((hi - lo + 1).astype(F32))
    cnt = by_group(cnts)
    diff = (sums / cnt - cur).astype(BF16)
    pool = jnp.dot(diff, wpool_ref[...], preferred_element_type=F32) * spool_ref[...]
    pool_o[0] = pool.astype(pool_o.dtype)

    z = _gelu_tanh(sg_ref[0].astype(F32))
    u = z[:, :GROUP_WIDTH]
    v = z[:, GROUP_WIDTH:]
    v = v * lax.rsqrt(jnp.mean(v * v, axis=-1, keepdims=True) + EPS) * gsgu_ref[...]
    v = v.astype(BF16)
    hmasks = [(grp == h).astype(BF16) for h in range(N_HEADS)]
    for c in range(tm // SGU_CHUNK):
        rows = slice(c * SGU_CHUNK, (c + 1) * SGU_CHUNK)
        vc = v[rows]
        vstack = jnp.concatenate([vc * hm for hm in hmasks], axis=0)
        s = jnp.dot(wsgu_ref[...], vstack, preferred_element_type=F32) + bsgu_ref[...]
        sgu_o[0, rows, :] = (u[rows] * s).astype(sgu_o.dtype)


def _mixers_call(proj, wpool_bd, spool, gsgu, wsgu_cat, bsgu_full, *, tm):
    B, L, _ = proj.shape
    hb = tm // POOL_HALO
    n_halo = L // POOL_HALO
    return pl.pallas_call(
        functools.partial(_mixers_kernel, tm=tm, seq_len=L),
        grid=(B, L // tm),
        in_specs=[pl.BlockSpec((1, tm, GROUP_WIDTH), lambda b, i: (b, i, COL_POOL)),
                  pl.BlockSpec((1, POOL_HALO, GROUP_WIDTH),
                               lambda b, i: (b, jnp.maximum(i * hb - 1, 0), COL_POOL)),
                  pl.BlockSpec((1, POOL_HALO, GROUP_WIDTH),
                               lambda b, i: (b, jnp.minimum((i + 1) * hb, n_halo - 1), COL_POOL)),
                  pl.BlockSpec((1, tm, 2 * GROUP_WIDTH), lambda b, i: (b, i, COL_SGU // 2)),
                  pl.BlockSpec((GROUP_WIDTH, GROUP_WIDTH), lambda b, i: (0, 0)),
                  pl.BlockSpec((1, GROUP_WIDTH), lambda b, i: (0, 0)),
                  pl.BlockSpec((1, GROUP_WIDTH), lambda b, i: (0, 0)),
                  pl.BlockSpec((SGU_CHUNK, N_HEADS * SGU_CHUNK), lambda b, i: (0, 0)),
                  pl.BlockSpec((SGU_CHUNK, GROUP_WIDTH), lambda b, i: (0, 0))],
        out_specs=[pl.BlockSpec((1, tm, GROUP_WIDTH), lambda b, i: (b, i, 0)),
                   pl.BlockSpec((1, tm, GROUP_WIDTH), lambda b, i: (b, i, 0))],
        out_shape=[jax.ShapeDtypeStruct((B, L, GROUP_WIDTH), BF16),
                   jax.ShapeDtypeStruct((B, L, GROUP_WIDTH), BF16)],
        compiler_params=_cparams(("arbitrary", "arbitrary")),
        name="pool_sgu",
    )(proj, proj, proj, proj, wpool_bd, spool, gsgu, wsgu_cat, bsgu_full)


def _outproj_kernel(cnt0_ref, p0, p1, p2, p3, w_ref, x_ref, gt_ref, g_ref, sc_ref, sh_ref,
                    wr_ref, br_ref, xo_ref, hf_ref, idx_ref, gate_ref, rank_ref, cnt_ref, cnt_scr):
    first = (pl.program_id(0) == 0) & (pl.program_id(1) == 0)

    @pl.when(first)
    def _():
        cnt_scr[...] = cnt0_ref[...]

    acc = None
    for j, p in enumerate((p0, p1, p2, p3)):
        part = jnp.dot(p[0], w_ref[j * GROUP_WIDTH:(j + 1) * GROUP_WIDTH, :],
                       preferred_element_type=F32)
        acc = part if acc is None else acc + part
    x = x_ref[0] + gt_ref[0] * acc
    xo_ref[0] = x
    ms = jnp.mean(x * x, axis=-1, keepdims=True)
    hf = (x * lax.rsqrt(ms + EPS) * g_ref[...]) * (1.0 + sc_ref[0]) + sh_ref[0]
    hf_ref[0] = hf

    tm = x.shape[0]
    logits = jnp.dot(hf, wr_ref[...], preferred_element_type=F32,
                     precision=lax.Precision.HIGHEST) + br_ref[...]
    lane_e = lax.broadcasted_iota(jnp.int32, (tm, N_EXPERTS), 1).astype(F32)
    vals, idxs = [], []
    l = logits
    for _ in range(TOP_K):
        mx = jnp.max(l, axis=-1, keepdims=True)
        ik = jnp.min(jnp.where(l == mx, lane_e, float(N_EXPERTS)), axis=-1, keepdims=True)
        vals.append(mx)
        idxs.append(ik)
        l = jnp.where(lane_e == ik, -jnp.inf, l)
    exps = [jnp.exp(v - vals[0]) for v in vals]
    denom = exps[0] + exps[1] + exps[2] + exps[3]
    gates = [e / denom for e in exps]

    onehot = jnp.zeros((tm, N_EXPERTS), F32)
    for ik in idxs:
        onehot = onehot + (lane_e == ik).astype(F32)
    ri = lax.broadcasted_iota(jnp.int32, (tm, tm), 0)
    ci = lax.broadcasted_iota(jnp.int32, (tm, tm), 1)
    tri = (ri > ci).astype(BF16)
    before = jnp.dot(tri, onehot.astype(BF16), preferred_element_type=F32) + cnt_scr[...]
    ranks = [jnp.sum(jnp.where(lane_e == ik, before, 0.0), axis=-1, keepdims=True) for ik in idxs]
    cnt_scr[...] = cnt_scr[...] + jnp.sum(onehot, axis=0, keepdims=True)
    cnt_ref[...] = cnt_scr[...]

    lane = lax.broadcasted_iota(jnp.int32, (tm, 128), 1)

    def pack(cols):
        o = jnp.zeros((tm, 128), F32)
        for k, cvals in enumerate(cols):
            o = jnp.where(lane == k, cvals, o)
        return o

    idx_ref[0] = pack(idxs).astype(jnp.int32)
    gate_ref[0] = pack(gates)
    rank_ref[0] = pack(ranks).astype(jnp.int32)


def _outproj_call(cnt0, parts, w_out_bf16, x, gt, g, sc, sh, w_router, b_router, *, tm):
    B, L, D = x.shape
    per_batch = gt.shape[0] > 1
    mod_map = (lambda b, i: (b, 0, 0)) if per_batch else (lambda b, i: (0, 0, 0))
    part_spec = pl.BlockSpec((1, tm, GROUP_WIDTH), lambda b, i: (b, i, 0))
    row_spec = pl.BlockSpec((1, tm, D), lambda b, i: (b, i, 0))
    meta_spec = pl.BlockSpec((1, tm, 128), lambda b, i: (b, i, 0))
    cnt_spec = pl.BlockSpec((1, N_EXPERTS), lambda b, i: (0, 0))
    return pl.pallas_call(
        _outproj_kernel,
        grid=(B, L // tm),
        in_specs=[cnt_spec, part_spec, part_spec, part_spec, part_spec,
                  pl.BlockSpec((D, D), lambda b, i: (0, 0)),
                  row_spec,
                  pl.BlockSpec((1, 1, D), mod_map),
                  pl.BlockSpec((1, D), lambda b, i: (0, 0)),
                  pl.BlockSpec((1, 1, D), mod_map),
                  pl.BlockSpec((1, 1, D), mod_map),
                  pl.BlockSpec((D, N_EXPERTS), lambda b, i: (0, 0)),
                  cnt_spec],
        out_specs=[row_spec, row_spec, meta_spec, meta_spec, meta_spec, cnt_spec],
        out_shape=[jax.ShapeDtypeStruct((B, L, D), F32),
                   jax.ShapeDtypeStruct((B, L, D), F32),
                   jax.ShapeDtypeStruct((B, L, 128), jnp.int32),
                   jax.ShapeDtypeStruct((B, L, 128), F32),
                   jax.ShapeDtypeStruct((B, L, 128), jnp.int32),
                   jax.ShapeDtypeStruct((1, N_EXPERTS), F32)],
        scratch_shapes=[pltpu.VMEM((1, N_EXPERTS), F32)],
        compiler_params=_cparams(("arbitrary", "arbitrary")),
        name="out_proj_router",
    )(cnt0, *parts, w_out_bf16, x, gt, g.reshape(1, D), sc, sh, w_router,
      b_router.reshape(1, N_EXPERTS))


def _dispatch_kernel(dest_ref, hf_ref, xs_in, xs_out, sem, *, tm):
    del xs_in

    def body(i, carry):
        for k in range(TOP_K):
            d = dest_ref[0, 0, i * TOP_K + k]
            pltpu.make_async_copy(hf_ref.at[pl.ds(i, 1)], xs_out.at[pl.ds(d, 1)], sem).start()
        return carry

    lax.fori_loop(0, tm, body, 0)
    for k in range(TOP_K):
        pltpu.make_async_copy(hf_ref, xs_out.at[pl.ds(0, tm)], sem).wait()


def _dispatch_call(dest, hf2d, xs, *, tm):
    T, D = hf2d.shape
    n_tiles = T // tm
    return pl.pallas_call(
        functools.partial(_dispatch_kernel, tm=tm),
        grid=(n_tiles,),
        in_specs=[pl.BlockSpec((1, 1, tm * TOP_K), lambda i: (i, 0, 0), memory_space=pltpu.SMEM),
                  pl.BlockSpec((tm, D), lambda i: (i, 0)),
                  pl.BlockSpec(memory_space=pl.ANY)],
        out_specs=pl.BlockSpec(memory_space=pl.ANY),
        out_shape=jax.ShapeDtypeStruct(xs.shape, xs.dtype),
        scratch_shapes=[pltpu.SemaphoreType.DMA(())],
        input_output_aliases={2: 0},
        compiler_params=_cparams(("arbitrary",)),
        name="moe_dispatch",
    )(dest.reshape(n_tiles, 1, tm * TOP_K), hf2d, xs)


def _expert_kernel(be_ref, na_ref, xs_ref, wgu_ref, bgu_ref, wd_ref, bd_ref, y_ref):
    del be_ref

    @pl.when(pl.program_id(0) < na_ref[0])
    def _():
        xb = xs_ref[...].astype(BF16)
        gu = jnp.dot(xb, wgu_ref[0], preferred_element_type=F32) + bgu_ref[0]
        gt = jnp.minimum(gu[:, :D_EXPERT], SWIGLU_LIMIT)
        up = jnp.clip(gu[:, D_EXPERT:], -SWIGLU_LIMIT, SWIGLU_LIMIT)
        act = (up + 1.0) * gt * (1.0 / (1.0 + jnp.exp(-SWIGLU_ALPHA * gt)))
        y_ref[...] = jnp.dot(act.astype(BF16), wd_ref[0], preferred_element_type=F32) + bd_ref[0]

    @pl.when(pl.program_id(0) >= na_ref[0])
    def _():
        y_ref[...] = jnp.zeros(y_ref.shape, y_ref.dtype)


def _expert_call(blk_e, n_active, xs, wgu, bgu, wd, bd):
    n_slots, D = xs.shape
    nb = n_slots // MOE_BM

    def row_map(i, be, na):
        return (jnp.minimum(i, na[0] - 1), 0)

    def w_map(i, be, na):
        return (be[jnp.minimum(i, na[0] - 1)], 0, 0)

    return pl.pallas_call(
        _expert_kernel,
        grid_spec=pltpu.PrefetchScalarGridSpec(
            num_scalar_prefetch=2,
            grid=(nb,),
            in_specs=[pl.BlockSpec((MOE_BM, D), row_map),
                      pl.BlockSpec((1, D, 2 * D_EXPERT), w_map),
                      pl.BlockSpec((1, 1, 2 * D_EXPERT), w_map),
                      pl.BlockSpec((1, D_EXPERT, D), w_map),
                      pl.BlockSpec((1, 1, D), w_map)],
            out_specs=pl.BlockSpec((MOE_BM, D), lambda i, be, na: (i, 0))),
        out_shape=jax.ShapeDtypeStruct((n_slots, D), F32),
        compiler_params=_cparams(("arbitrary",)),
        name="moe_experts",
    )(blk_e, n_active, xs, wgu, bgu.reshape(N_EXPERTS, 1, 2 * D_EXPERT), wd,
      bd.reshape(N_EXPERTS, 1, D))


def _combine_kernel(dest_ref, gate_ref, x_ref, gt_ref, gfin_ref, yb_ref, o_ref, buf, sem, *,
                    tm, final_norm):
    def body(i, carry):
        for k in range(TOP_K):
            d = dest_ref[0, 0, i * TOP_K + k]
            pltpu.make_async_copy(yb_ref.at[pl.ds(d, 1)], buf.at[k, pl.ds(i, 1)], sem).start()
        return carry

    lax.fori_loop(0, tm, body, 0)
    for k in range(TOP_K):
        pltpu.make_async_copy(yb_ref.at[pl.ds(0, tm)], buf.at[k], sem).wait()
    gates = gate_ref[0]
    y = buf[0] * gates[:, 0:1]
    for k in range(1, TOP_K):
        y = y + buf[k] * gates[:, k:k + 1]
    x = x_ref[0] + gt_ref[0] * y
    if final_norm:
        ms = jnp.mean(x * x, axis=-1, keepdims=True)
        x = x * lax.rsqrt(ms + EPS) * gfin_ref[...]
    o_ref[0] = x


def _combine_call(dest, gates, x, gt, g_final, yb, *, tm, final_norm):
    B, L, D = x.shape
    n_l = L // tm
    per_batch = gt.shape[0] > 1
    mod_map = (lambda b, i: (b, 0, 0)) if per_batch else (lambda b, i: (0, 0, 0))
    return pl.pallas_call(
        functools.partial(_combine_kernel, tm=tm, final_norm=final_norm),
        grid=(B, n_l),
        in_specs=[pl.BlockSpec((1, 1, tm * TOP_K), lambda b, i: (b * n_l + i, 0, 0),
                               memory_space=pltpu.SMEM),
                  pl.BlockSpec((1, tm, 128), lambda b, i: (b, i, 0)),
                  pl.BlockSpec((1, tm, D), lambda b, i: (b, i, 0)),
                  pl.BlockSpec((1, 1, D), mod_map),
                  pl.BlockSpec((1, D), lambda b, i: (0, 0)),
                  pl.BlockSpec(memory_space=pl.ANY)],
        out_specs=pl.BlockSpec((1, tm, D), lambda b, i: (b, i, 0)),
        out_shape=jax.ShapeDtypeStruct((B, L, D), F32),
        scratch_shapes=[pltpu.VMEM((TOP_K, tm, D), F32), pltpu.SemaphoreType.DMA(())],
        compiler_params=_cparams(("arbitrary", "arbitrary")),
        name="moe_combine",
    )(dest.reshape(B * n_l, 1, tm * TOP_K), gates, x, gt, g_final.reshape(1, D), yb)


def _moe(streams, p_bf16, b_gu, b_down, g_final, final_norm):
    counts = streams[-1]["cnt"].reshape(N_EXPERTS).astype(jnp.int32)
    padded = (counts + MOE_BM - 1) // MOE_BM * MOE_BM
    cum_pad = jnp.cumsum(padded)
    pstart = cum_pad - padded
    total_tokens = sum(s["x"].shape[0] * s["x"].shape[1] for s in streams)
    nb = -(-total_tokens * TOP_K // MOE_BM) + N_EXPERTS
    blk_e = jnp.minimum(jnp.sum(cum_pad[None, :] <= jnp.arange(nb)[:, None] * MOE_BM, axis=1),
                        N_EXPERTS - 1).astype(jnp.int32)
    expert_ids = jnp.arange(N_EXPERTS, dtype=jnp.int32)
    n_active = (cum_pad[-1] // MOE_BM).astype(jnp.int32).reshape(1)

    xs = jnp.zeros((nb * MOE_BM, D_MODEL), F32)
    dests = []
    for s in streams:
        B, L, _ = s["x"].shape
        idx = s["idx"][..., :TOP_K]
        seg = jnp.sum(jnp.where(idx[..., None] == expert_ids, pstart, 0), axis=-1)
        dest = (seg + s["rank"][..., :TOP_K]).reshape(B * L * TOP_K)
        dests.append(dest)
        xs = _dispatch_call(dest, s["hf"].reshape(B * L, D_MODEL), xs, tm=min(256, L))
    yb = _expert_call(blk_e, n_active, xs, p_bf16["w_gu"], b_gu, p_bf16["w_down"], b_down)
    outs = []
    for s, dest in zip(streams, dests):
        L = s["x"].shape[1]
        outs.append(_combine_call(dest, s["gate"], s["x"], s["gt"], g_final, yb,
                                  tm=min(128, L), final_norm=final_norm))
    return outs


def _layer(x, xc, c8, p, layer_idx, update_ctx, tabs, g_final, final_norm):
    B, S, D = x.shape
    C = xc.shape[1]
    lam_init = 0.8 - 0.6 * math.exp(-0.3 * layer_idx)
    lam = (jnp.exp(jnp.sum(p["lam_q1"] * p["lam_k1"])) - jnp.exp(jnp.sum(p["lam_q2"] * p["lam_k2"]))
           + lam_init).reshape(1).astype(F32)

    mod8 = _mod_call(c8, p["w_mod"], p["b_mod"])
    mod = [m.reshape(B, 1, D) for m in jnp.split(mod8[:B], 6, axis=-1)]
    modc = [m.reshape(1, 1, D) for m in jnp.split(mod8[B:B + 1], 6, axis=-1)]
    sh1, sc1, gt1, sh2, sc2, gt2 = mod

    w_in = p["w_in"].astype(BF16)
    w_out = p["w_out"].astype(BF16)
    proj = _inproj_call(x, sc1, sh1, p["g_mix"], w_in, tabs["cos"], tabs["sin"], rope=True, tm=512)
    projc = _inproj_call(xc, modc[1], modc[0], p["g_mix"], w_in, tabs["cos"], tabs["sin"],
                         rope=False, tm=C)

    gsub256 = jnp.tile(p["g_sub"], N_HEADS).reshape(1, GROUP_WIDTH)
    oa = _da_call(lam, proj, proj, projc, gsub256, tabs["gmat"], lam_init=lam_init, tq=256, tkc=512)
    bias = _na_bias_tables(p["rpb"], S // GRID_W)
    od = _na_call(proj, projc, bias)

    wpool_bd = jax.scipy.linalg.block_diag(*[p["w_pool"][g] for g in range(4)]).astype(BF16)
    spool = p["s_pool"].reshape(1, GROUP_WIDTH)
    gsgu = p["g_sgu"].reshape(1, GROUP_WIDTH)
    wsgu_cat = jnp.transpose(p["w_sgu"], (1, 0, 2)).reshape(SGU_CHUNK, N_HEADS * SGU_CHUNK).astype(BF16)
    bsgu_full = jnp.repeat(p["b_sgu"].T, HEAD_DIM, axis=1)
    pool, sgu = _mixers_call(proj, wpool_bd, spool, gsgu, wsgu_cat, bsgu_full, tm=512)

    cnt0 = jnp.zeros((1, N_EXPERTS), F32)
    x1, hf, idx, gate, rank, cnt = _outproj_call(
        cnt0, (oa, pool, sgu, od), w_out, x, gt1, p["g_ffn"], sc2, sh2,
        p["w_router"], p["b_router"], tm=256)
    streams = [dict(x=x1, hf=hf, idx=idx, gate=gate, rank=rank, gt=gt2, cnt=cnt)]

    if update_ctx:
        oa_c = _da_call(lam, projc, None, projc, gsub256, tabs["gmat"], lam_init=lam_init,
                        tq=C, tkc=512)
        od_c = _ctx_attn_call(projc)
        pool_c, sgu_c = _mixers_call(projc, wpool_bd, spool, gsgu, wsgu_cat, bsgu_full, tm=C)
        xc1, hfc, idxc, gatec, rankc, cnt = _outproj_call(
            cnt, (oa_c, pool_c, sgu_c, od_c), w_out, xc, modc[2], p["g_ffn"], modc[4], modc[3],
            p["w_router"], p["b_router"], tm=C)
        streams.append(dict(x=xc1, hf=hfc, idx=idxc, gate=gatec, rank=rankc, gt=modc[5], cnt=cnt))

    p_bf16 = dict(w_gu=p["w_gu"].astype(BF16), w_down=p["w_down"].astype(BF16))
    outs = _moe(streams, p_bf16, p["b_gu"], p["b_down"], g_final, final_norm)
    if update_ctx:
        return outs[0], outs[1]
    return outs[0], xc


def kernel(x, c, ctx, c_ctx, w_mod, b_mod, g_mix, g_ffn, w_in, w_out, lam_q1, lam_k1, lam_q2,
           lam_k2, g_sub, w_pool, s_pool, g_sgu, w_sgu, b_sgu, rpb, w_router, b_router, w_gu,
           b_gu, w_down, b_down, g_final):
    B, S, D = x.shape
    cos_tab, sin_tab = _rope_tables(S)
    gidx = np.arange(GROUP_WIDTH) // HEAD_DIM
    gmat = jnp.asarray((gidx[:, None] == gidx[None, :]).astype(np.float32) / HEAD_DIM)
    tabs = dict(cos=cos_tab, sin=sin_tab, gmat=gmat)
    c8 = jnp.concatenate([c, c_ctx[None, :], jnp.zeros((8 - B - 1, D), F32)], axis=0)
    xc = ctx
    for l in range(DEPTH):
        p = dict(w_mod=w_mod[l], b_mod=b_mod[l], g_mix=g_mix[l], g_ffn=g_ffn[l], w_in=w_in[l],
                 w_out=w_out[l], lam_q1=lam_q1[l], lam_k1=lam_k1[l], lam_q2=lam_q2[l],
                 lam_k2=lam_k2[l], g_sub=g_sub[l], w_pool=w_pool[l], s_pool=s_pool[l],
                 g_sgu=g_sgu[l], w_sgu=w_sgu[l], b_sgu=b_sgu[l], rpb=rpb[l],
                 w_router=w_router[l], b_router=b_router[l], w_gu=w_gu[l], b_gu=b_gu[l],
                 w_down=w_down[l], b_down=b_down[l])
        x, xc = _layer(x, xc, c8, p, l, l < DEPTH - 1, tabs, g_final, l == DEPTH - 1)
    return x
```

```python
import functools
import math

import jax
import jax.numpy as jnp
import numpy as np
from jax import lax
from jax.experimental import pallas as pl
from jax.experimental.pallas import tpu as pltpu

F32 = jnp.float32
BF16 = jnp.bfloat16

D_MODEL = 1024
DEPTH = 2
GRID_W = 64
HEAD_DIM = 64
N_HEADS = 4
GROUP_WIDTH = N_HEADS * HEAD_DIM
PROJ_WIDTH = 9 * GROUP_WIDTH
DA_QK = HEAD_DIM // 2
ROPE_BASE = 10000.0
POOL_WINDOWS = (2, 4, 8, 16)
SGU_CHUNK = 128
WIN_R = 8
WIN_C = 16
N_EXPERTS = 32
TOP_K = 4
D_EXPERT = D_MODEL
SWIGLU_LIMIT = 7.0
SWIGLU_ALPHA = 1.702
EPS = 1e-6
NEG_INF = -1e30
LOG2_E = math.log2(math.e)

COL_AQ, COL_AK, COL_AV, COL_POOL, COL_SGU, COL_NQ, COL_NK, COL_NV = 0, 1, 2, 3, 4, 6, 7, 8

NA_QROWS = 4
NA_BAND = 12
POOL_HALO = 16
MOE_BM = 256
VMEM_LIMIT = 48 * 1024 * 1024


def _cparams(sem):
    return pltpu.CompilerParams(dimension_semantics=sem, vmem_limit_bytes=VMEM_LIMIT)


def _lane_iota(n):
    return lax.broadcasted_iota(jnp.int32, (1, n), 1)


def _dot_nt(a, b):
    return lax.dot_general(a, b, (((1,), (1,)), ((), ())), preferred_element_type=F32)


def _mod_kernel(c_ref, w_ref, b_ref, o_ref):
    c = c_ref[...]
    s = c * (1.0 / (1.0 + jnp.exp(-c)))
    o_ref[...] = jnp.dot(s.astype(BF16), w_ref[...].astype(BF16),
                         preferred_element_type=F32) + b_ref[...]


def _mod_call(c8, w_mod, b_mod):
    n = w_mod.shape[1]
    tn = 1536
    return pl.pallas_call(
        _mod_kernel,
        grid=(n // tn,),
        in_specs=[pl.BlockSpec((8, D_MODEL), lambda j: (0, 0)),
                  pl.BlockSpec((D_MODEL, tn), lambda j: (0, j)),
                  pl.BlockSpec((1, tn), lambda j: (0, j))],
        out_specs=pl.BlockSpec((8, tn), lambda j: (0, j)),
        out_shape=jax.ShapeDtypeStruct((8, n), F32),
        compiler_params=_cparams(("arbitrary",)),
        name="adaln_mod",
    )(c8, w_mod, b_mod.reshape(1, n))


def _swap8(x):
    n = x.shape[-1]
    lane = _lane_iota(n)
    first_half = (lane & 15) < 8
    return jnp.where(first_half, pltpu.roll(x, n - 8, axis=1), pltpu.roll(x, 8, axis=1))


def _inproj_kernel(x_ref, sc_ref, sh_ref, g_ref, w_ref, cos_ref, sin_ref, o_ref, *, rope):
    x = x_ref[0]
    ms = jnp.mean(x * x, axis=-1, keepdims=True)
    y = x * lax.rsqrt(ms + EPS) * g_ref[...]
    h = (y * (1.0 + sc_ref[0]) + sh_ref[0]).astype(BF16)
    for j in range(PROJ_WIDTH // GROUP_WIDTH):
        cols = slice(j * GROUP_WIDTH, (j + 1) * GROUP_WIDTH)
        acc = jnp.dot(h, w_ref[:, cols], preferred_element_type=F32)
        if rope and j in (COL_AQ, COL_AK):
            acc = acc * cos_ref[...] + _swap8(acc) * sin_ref[...]
        if j == COL_AQ:
            acc = acc * (DA_QK ** -0.5 * LOG2_E)
        if j == COL_NQ:
            acc = acc * (HEAD_DIM ** -0.5)
        o_ref[0, :, cols] = acc.astype(o_ref.dtype)


def _inproj_call(x, sc, sh, g, w_bf16, cos_tab, sin_tab, *, rope, tm):
    B, L, D = x.shape
    per_batch = sc.shape[0] > 1
    mod_map = (lambda b, i: (b, 0, 0)) if per_batch else (lambda b, i: (0, 0, 0))
    tab_map = (lambda b, i: (i, 0)) if rope else (lambda b, i: (0, 0))
    return pl.pallas_call(
        functools.partial(_inproj_kernel, rope=rope),
        grid=(B, L // tm),
        in_specs=[pl.BlockSpec((1, tm, D), lambda b, i: (b, i, 0)),
                  pl.BlockSpec((1, 1, D), mod_map),
                  pl.BlockSpec((1, 1, D), mod_map),
                  pl.BlockSpec((1, D), lambda b, i: (0, 0)),
                  pl.BlockSpec((D, PROJ_WIDTH), lambda b, i: (0, 0)),
                  pl.BlockSpec((tm, GROUP_WIDTH), tab_map),
                  pl.BlockSpec((tm, GROUP_WIDTH), tab_map)],
        out_specs=pl.BlockSpec((1, tm, PROJ_WIDTH), lambda b, i: (b, i, 0)),
        out_shape=jax.ShapeDtypeStruct((B, L, PROJ_WIDTH), BF16),
        compiler_params=_cparams(("arbitrary", "arbitrary")),
        name="in_proj",
    )(x, sc, sh, g.reshape(1, D), w_bf16, cos_tab, sin_tab)


def _rope_tables(S):
    n = 8
    inv = ROPE_BASE ** (-jnp.arange(n, dtype=F32) / n)
    t = jnp.arange(S)
    row_pos = (t // GRID_W).astype(F32)
    col_pos = (t % GRID_W).astype(F32)
    parts_c, parts_s = [], []
    for pos in (row_pos, col_pos):
        ang = pos[:, None] * inv[None, :]
        c, s = jnp.cos(ang), jnp.sin(ang)
        parts_c += [c, c]
        parts_s += [-s, s]
    cos32 = jnp.concatenate(parts_c, axis=1)
    sin32 = jnp.concatenate(parts_s, axis=1)
    reps = GROUP_WIDTH // 32
    return jnp.tile(cos32, (1, reps)), jnp.tile(sin32, (1, reps))


def _da_kernel(lam_ref, *refs, n_lat, tkc, post_scale):
    if n_lat:
        q_ref, kl_ref, vlt_ref, kc_ref, vct_ref, gsub_ref, o_ref = refs[:7]
    else:
        q_ref, kc_ref, vct_ref, gsub_ref, o_ref = refs[:5]
    qs_scr, m_scr, l_scr, acc_scr = refs[-4:]
    q = q_ref[0]
    tq = q.shape[0]
    lane = _lane_iota(GROUP_WIDTH)
    for g in range(2 * N_HEADS):
        sel = (lane >= g * DA_QK) & (lane < (g + 1) * DA_QK)
        qs_scr[g * tq:(g + 1) * tq, :] = q * sel.astype(BF16)
    m_scr[...] = jnp.full(m_scr.shape, NEG_INF, F32)
    l_scr[...] = jnp.zeros(l_scr.shape, F32)
    acc_scr[...] = jnp.zeros(acc_scr.shape, F32)

    def step(k, vt):
        st = _dot_nt(k, qs_scr[...])
        m_old = m_scr[...]
        m_new = jnp.maximum(m_old, jnp.max(st, axis=0, keepdims=True))
        alpha = jnp.exp2(m_old - m_new)
        p = jnp.exp2(st - m_new)
        l_scr[...] = alpha * l_scr[...] + jnp.sum(p, axis=0, keepdims=True)
        m_scr[...] = m_new
        pb = p.astype(BF16)
        for h in range(N_HEADS):
            rows = slice(h * HEAD_DIM, (h + 1) * HEAD_DIM)
            cols = slice(2 * h * tq, (2 * h + 2) * tq)
            pv = jnp.dot(vt[rows, :], pb[:, cols], preferred_element_type=F32)
            acc_scr[rows, :] = alpha[:, cols] * acc_scr[rows, :] + pv

    if n_lat:
        def body(c, carry):
            off = pl.multiple_of(c * tkc, tkc)
            step(kl_ref[0, pl.ds(off, tkc), :], vlt_ref[0, c])
            return carry
        lax.fori_loop(0, n_lat, body, 0)
    step(kc_ref[0], vct_ref[0, 0])

    lam = lam_ref[0]
    outs = []
    for h in range(N_HEADS):
        rows = slice(h * HEAD_DIM, (h + 1) * HEAD_DIM)
        cols = slice(2 * h * tq, (2 * h + 2) * tq)
        o = acc_scr[rows, :] / l_scr[:, cols]
        o_h = o[:, :tq] - lam * o[:, tq:]
        ms = jnp.mean(o_h * o_h, axis=0, keepdims=True)
        outs.append(o_h * lax.rsqrt(ms + EPS) * gsub_ref[...] * post_scale)
    o_ref[0] = jnp.concatenate(outs, axis=0).T.astype(o_ref.dtype)


def _values_t(proj, tkc):
    B, L, _ = proj.shape
    v = proj[:, :, COL_AV * GROUP_WIDTH:(COL_AV + 1) * GROUP_WIDTH]
    return jnp.transpose(v.reshape(B, L // tkc, tkc, GROUP_WIDTH), (0, 1, 3, 2))


def _da_call(lam, proj_q, proj_lat, proj_ctx, gsub_col, *, lam_init, tq, tkc):
    B, Lq, _ = proj_q.shape
    C = proj_ctx.shape[1]
    in_specs = [pl.BlockSpec(memory_space=pltpu.SMEM),
                pl.BlockSpec((1, tq, GROUP_WIDTH), lambda b, i: (b, i, COL_AQ))]
    args = [lam, proj_q]
    n_lat = 0
    if proj_lat is not None:
        S = proj_lat.shape[1]
        n_lat = S // tkc
        in_specs += [pl.BlockSpec((1, S, GROUP_WIDTH), lambda b, i: (b, 0, COL_AK)),
                     pl.BlockSpec((1, n_lat, GROUP_WIDTH, tkc), lambda b, i: (b, 0, 0, 0))]
        args += [proj_lat, _values_t(proj_lat, tkc)]
    in_specs += [pl.BlockSpec((1, C, GROUP_WIDTH), lambda b, i: (b, 0, COL_AK)),
                 pl.BlockSpec((1, 1, GROUP_WIDTH, C), lambda b, i: (b, 0, 0, 0)),
                 pl.BlockSpec((HEAD_DIM, 1), lambda b, i: (0, 0))]
    args += [proj_ctx, _values_t(proj_ctx, C), gsub_col]
    n_stack = 2 * N_HEADS * tq
    return pl.pallas_call(
        functools.partial(_da_kernel, n_lat=n_lat, tkc=tkc, post_scale=1.0 - lam_init),
        grid=(B, Lq // tq),
        in_specs=in_specs,
        out_specs=pl.BlockSpec((1, tq, GROUP_WIDTH), lambda b, i: (b, i, 0)),
        out_shape=jax.ShapeDtypeStruct((B, Lq, GROUP_WIDTH), BF16),
        scratch_shapes=[pltpu.VMEM((n_stack, GROUP_WIDTH), BF16),
                        pltpu.VMEM((1, n_stack), F32),
                        pltpu.VMEM((1, n_stack), F32),
                        pltpu.VMEM((GROUP_WIDTH, 2 * tq), F32)],
        compiler_params=_cparams(("arbitrary", "arbitrary")),
        name="diff_attn" if n_lat else "diff_attn_ctx",
    )(*args)


def _softmax_heads(q, sources, out_dtype):
    tq = q.shape[0]
    lane = _lane_iota(GROUP_WIDTH)
    out = jnp.zeros((tq, GROUP_WIDTH), F32)
    for h in range(N_HEADS):
        hm = (lane >= h * HEAD_DIM) & (lane < (h + 1) * HEAD_DIM)
        qm = q * hm.astype(BF16)
        scores = []
        for k, _, bias_fn in sources:
            s = _dot_nt(qm, k)
            if bias_fn is not None:
                s = s + bias_fn(h)
            scores.append(s)
        m = scores[0].max(axis=-1, keepdims=True)
        for s in scores[1:]:
            m = jnp.maximum(m, s.max(axis=-1, keepdims=True))
        l = jnp.zeros((tq, 1), F32)
        o = jnp.zeros((tq, GROUP_WIDTH), F32)
        for s, (_, v, _) in zip(scores, sources):
            p = jnp.exp(s - m)
            l = l + jnp.sum(p, axis=-1, keepdims=True)
            o = o + jnp.dot(p.astype(BF16), v, preferred_element_type=F32)
        out = jnp.where(hm, o / l, out)
    return out.astype(out_dtype)


def _na_kernel(q_ref, kn_ref, vn_ref, kc_ref, vc_ref, bias_ref, o_ref, *, n_groups):
    g = pl.program_id(1)
    tq = NA_QROWS * GRID_W
    band = NA_BAND * GRID_W
    start = pl.multiple_of(jnp.clip(g - 1, 0, n_groups - NA_BAND // NA_QROWS) * tq, tq)
    kb = kn_ref[0, pl.ds(start, band), :]
    vb = vn_ref[0, pl.ds(start, band), :]
    sources = [(kb, vb, lambda h: bias_ref[0, h]), (kc_ref[0], vc_ref[0], None)]
    o_ref[0] = _softmax_heads(q_ref[0], sources, o_ref.dtype)


def _na_call(proj, proj_ctx, bias):
    B, S, _ = proj.shape
    C = proj_ctx.shape[1]
    tq = NA_QROWS * GRID_W
    band = NA_BAND * GRID_W
    n_groups = S // tq

    def bias_map(b, g):
        return (jnp.where(g == 0, 0, jnp.where(g == n_groups - 1, 2, 1)), 0, 0, 0)

    return pl.pallas_call(
        functools.partial(_na_kernel, n_groups=n_groups),
        grid=(B, n_groups),
        in_specs=[pl.BlockSpec((1, tq, GROUP_WIDTH), lambda b, g: (b, g, COL_NQ)),
                  pl.BlockSpec((1, S, GROUP_WIDTH), lambda b, g: (b, 0, COL_NK)),
                  pl.BlockSpec((1, S, GROUP_WIDTH), lambda b, g: (b, 0, COL_NV)),
                  pl.BlockSpec((1, C, GROUP_WIDTH), lambda b, g: (b, 0, COL_NK)),
                  pl.BlockSpec((1, C, GROUP_WIDTH), lambda b, g: (b, 0, COL_NV)),
                  pl.BlockSpec((1, N_HEADS, tq, band), bias_map)],
        out_specs=pl.BlockSpec((1, tq, GROUP_WIDTH), lambda b, g: (b, g, 0)),
        out_shape=jax.ShapeDtypeStruct((B, S, GROUP_WIDTH), BF16),
        compiler_params=_cparams(("arbitrary", "arbitrary")),
        name="nbr_attn",
    )(proj, proj, proj, proj_ctx, proj_ctx, bias)


def _na_bias_tables(rpb, rows):
    n_groups = rows // NA_QROWS
    n_rr, n_rc = 2 * WIN_R - 1, 2 * WIN_C - 1
    c = np.arange(GRID_W)[:, None]
    kc = np.arange(GRID_W)[None, :]
    col_lo = np.clip(c - WIN_C // 2, 0, GRID_W - WIN_C)
    col_ok = (kc >= col_lo) & (kc < col_lo + WIN_C)
    rel_c = np.clip(kc - c, -(WIN_C - 1), WIN_C - 1) + (WIN_C - 1)
    sel_c = (rel_c[..., None] == np.arange(n_rc)).astype(np.float32)
    sel_r, ok = [], []
    for g in (0, 1, n_groups - 1):
        r0 = g * NA_QROWS
        start = min(max(r0 - WIN_R // 2, 0), rows - NA_BAND)
        r = r0 + np.arange(NA_QROWS)[:, None]
        kr = start + np.arange(NA_BAND)[None, :]
        row_lo = np.clip(r - WIN_R // 2, 0, rows - WIN_R)
        row_ok = (kr >= row_lo) & (kr < row_lo + WIN_R)
        rel_r = kr - r + (WIN_R - 1)
        sel_r.append((rel_r[..., None] == np.arange(n_rr)).astype(np.float32))
        ok.append(row_ok[:, None, :, None] & col_ok[None, :, None, :])
    sel_r = jnp.asarray(np.stack(sel_r))
    ok = np.stack(ok).reshape(3, 1, NA_QROWS * GRID_W, NA_BAND * GRID_W)
    vals = jnp.einsum("vxka,hab,cyb->vhxcky", sel_r, rpb.astype(F32), jnp.asarray(sel_c),
                      precision=lax.Precision.HIGHEST)
    vals = vals.reshape(3, N_HEADS, NA_QROWS * GRID_W, NA_BAND * GRID_W)
    return jnp.where(jnp.asarray(ok), vals, NEG_INF)


def _ctx_attn_kernel(q_ref, k_ref, v_ref, o_ref):
    o_ref[0] = _softmax_heads(q_ref[0], [(k_ref[0], v_ref[0], None)], o_ref.dtype)


def _ctx_attn_call(proj_ctx):
    B, C, _ = proj_ctx.shape
    return pl.pallas_call(
        _ctx_attn_kernel,
        grid=(B,),
        in_specs=[pl.BlockSpec((1, C, GROUP_WIDTH), lambda b: (b, 0, COL_NQ)),
                  pl.BlockSpec((1, C, GROUP_WIDTH), lambda b: (b, 0, COL_NK)),
                  pl.BlockSpec((1, C, GROUP_WIDTH), lambda b: (b, 0, COL_NV))],
        out_specs=pl.BlockSpec((1, C, GROUP_WIDTH), lambda b: (b, 0, 0)),
        out_shape=jax.ShapeDtypeStruct((B, C, GROUP_WIDTH), BF16),
        compiler_params=_cparams(("arbitrary",)),
        name="ctx_attn",
    )(proj_ctx, proj_ctx, proj_ctx)


def _gelu_tanh(x):
    return 0.5 * x * (1.0 + jnp.tanh(math.sqrt(2.0 / math.pi) * (x + 0.044715 * (x * x * x))))


def _mixers_kernel(pc_ref, pp_ref, pn_ref, sg_ref, wpool_ref, spool_ref, gsgu_ref, wsgu_ref,
                   bsgu_ref, pool_o, sgu_o, *, tm, seq_len):
    i = pl.program_id(1)
    n_tiles = seq_len // tm
    lane = _lane_iota(GROUP_WIDTH)
    grp = lax.shift_right_logical(lane, 6)

    def by_group(vals):
        return jnp.where(grp == 0, vals[0], jnp.where(grp == 1, vals[1],
                                                       jnp.where(grp == 2, vals[2], vals[3])))

    cur = pc_ref[0].astype(F32)
    prev = jnp.where(i > 0, pp_ref[0].astype(F32), 0.0)
    nxt = jnp.where(i < n_tiles - 1, pn_ref[0].astype(F32), 0.0)
    ext = jnp.concatenate([prev, cur, nxt], axis=0)
    n = tm + 2 * POOL_HALO

    def ahead(x, d):
        return pltpu.roll(x, n - d, axis=0)

    def behind(x, d):
        return pltpu.roll(x, d, axis=0)

    f2 = ext + ahead(ext, 1)
    f4 = f2 + ahead(f2, 2)
    f8 = f4 + ahead(f4, 4)
    f16 = f8 + ahead(f8, 8)
    sums = by_group([behind(f2, 1), behind(f4, 2), behind(f8, 4), behind(f16, 8)])
    sums = sums[POOL_HALO:POOL_HALO + tm]
    t = i * tm + lax.broadcasted_iota(jnp.int32, (tm, 1), 0)
    cnts = []
    for w in POOL_WINDOWS:
        lo = jnp.maximum(t - w // 2, 0)
        hi = jnp.minimum(t + (w - w // 2 - 1), seq_len - 1)
        cnts.append((hi - lo + 1).astype(F32))
    cnt = by_group(cnts)
    diff = (sums / cnt - cur).astype(BF16)
    pool = jnp.dot(diff, wpool_ref[...], preferred_element_type=F32) * spool_ref[...]
    pool_o[0] = pool.astype(pool_o.dtype)

    z = _gelu_tanh(sg_ref[0].astype(F32))
    u = z[:, :GROUP_WIDTH]
    v = z[:, GROUP_WIDTH:]
    v = v * lax.rsqrt(jnp.mean(v * v, axis=-1, keepdims=True) + EPS) * gsgu_ref[...]
    v = v.astype(BF16)
    hmasks = [(grp == h).astype(BF16) for h in range(N_HEADS)]
    for c in range(tm // SGU_CHUNK):
        rows = slice(c * SGU_CHUNK, (c + 1) * SGU_CHUNK)
        vc = v[rows]
        vstack = jnp.concatenate([vc * hm for hm in hmasks], axis=0)
        s = jnp.dot(wsgu_ref[...], vstack, preferred_element_type=F32) + bsgu_ref[...]
        sgu_o[0, rows, :] = (u[rows] * s).astype(sgu_o.dtype)


def _mixers_call(proj, wpool_bd, spool, gsgu, wsgu_cat, bsgu_full, *, tm):
    B, L, _ = proj.shape
    hb = tm // POOL_HALO
    n_halo = L // POOL_HALO
    return pl.pallas_call(
        functools.partial(_mixers_kernel, tm=tm, seq_len=L),
        grid=(B, L // tm),
        in_specs=[pl.BlockSpec((1, tm, GROUP_WIDTH), lambda b, i: (b, i, COL_POOL)),
                  pl.BlockSpec((1, POOL_HALO, GROUP_WIDTH),
                               lambda b, i: (b, jnp.maximum(i * hb - 1, 0), COL_POOL)),
                  pl.BlockSpec((1, POOL_HALO, GROUP_WIDTH),
                               lambda b, i: (b, jnp.minimum((i + 1) * hb, n_halo - 1), COL_POOL)),
                  pl.BlockSpec((1, tm, 2 * GROUP_WIDTH), lambda b, i: (b, i, COL_SGU // 2)),
                  pl.BlockSpec((GROUP_WIDTH, GROUP_WIDTH), lambda b, i: (0, 0)),
                  pl.BlockSpec((1, GROUP_WIDTH), lambda b, i: (0, 0)),
                  pl.BlockSpec((1, GROUP_WIDTH), lambda b, i: (0, 0)),
                  pl.BlockSpec((SGU_CHUNK, N_HEADS * SGU_CHUNK), lambda b, i: (0, 0)),
                  pl.BlockSpec((SGU_CHUNK, GROUP_WIDTH), lambda b, i: (0, 0))],
        out_specs=[pl.BlockSpec((1, tm, GROUP_WIDTH), lambda b, i: (b, i, 0)),
                   pl.BlockSpec((1, tm, GROUP_WIDTH), lambda b, i: (b, i, 0))],
        out_shape=[jax.ShapeDtypeStruct((B, L, GROUP_WIDTH), BF16),
                   jax.ShapeDtypeStruct((B, L, GROUP_WIDTH), BF16)],
        compiler_params=_cparams(("arbitrary", "arbitrary")),
        name="pool_sgu",
    )(proj, proj, proj, proj, wpool_bd, spool, gsgu, wsgu_cat, bsgu_full)


def _outproj_kernel(cnt0_ref, p0, p1, p2, p3, w_ref, x_ref, gt_ref, g_ref, sc_ref, sh_ref,
                    wr_ref, br_ref, xo_ref, hf_ref, idx_ref, gate_ref, rank_ref, cnt_ref, cnt_scr):
    first = (pl.program_id(0) == 0) & (pl.program_id(1) == 0)

    @pl.when(first)
    def _():
        cnt_scr[...] = cnt0_ref[...]

    acc = None
    for j, p in enumerate((p0, p1, p2, p3)):
        part = jnp.dot(p[0], w_ref[j * GROUP_WIDTH:(j + 1) * GROUP_WIDTH, :],
                       preferred_element_type=F32)
        acc = part if acc is None else acc + part
    x = x_ref[0] + gt_ref[0] * acc
    xo_ref[0] = x
    ms = jnp.mean(x * x, axis=-1, keepdims=True)
    hf = (x * lax.rsqrt(ms + EPS) * g_ref[...]) * (1.0 + sc_ref[0]) + sh_ref[0]
    hf_ref[0] = hf

    tm = x.shape[0]
    logits = jnp.dot(hf, wr_ref[...], preferred_element_type=F32,
                     precision=lax.Precision.HIGHEST) + br_ref[...]
    lane_e = lax.broadcasted_iota(jnp.int32, (tm, N_EXPERTS), 1).astype(F32)
    vals, idxs = [], []
    l = logits
    for _ in range(TOP_K):
        mx = jnp.max(l, axis=-1, keepdims=True)
        ik = jnp.min(jnp.where(l == mx, lane_e, float(N_EXPERTS)), axis=-1, keepdims=True)
        vals.append(mx)
        idxs.append(ik)
        l = jnp.where(lane_e == ik, -jnp.inf, l)
    exps = [jnp.exp(v - vals[0]) for v in vals]
    denom = exps[0] + exps[1] + exps[2] + exps[3]
    gates = [e / denom for e in exps]

    onehot = jnp.zeros((tm, N_EXPERTS), F32)
    for ik in idxs:
        onehot = onehot + (lane_e == ik).astype(F32)
    ri = lax.broadcasted_iota(jnp.int32, (tm, tm), 0)
    ci = lax.broadcasted_iota(jnp.int32, (tm, tm), 1)
    tri = (ri > ci).astype(BF16)
    before = jnp.dot(tri, onehot.astype(BF16), preferred_element_type=F32) + cnt_scr[...]
    ranks = [jnp.sum(jnp.where(lane_e == ik, before, 0.0), axis=-1, keepdims=True) for ik in idxs]
    cnt_scr[...] = cnt_scr[...] + jnp.sum(onehot, axis=0, keepdims=True)
    cnt_ref[...] = cnt_scr[...]

    lane = lax.broadcasted_iota(jnp.int32, (tm, 128), 1)

    def pack(cols):
        o = jnp.zeros((tm, 128), F32)
        for k, cvals in enumerate(cols):
            o = jnp.where(lane == k, cvals, o)
        return o

    idx_ref[0] = pack(idxs).astype(jnp.int32)
    gate_ref[0] = pack(gates)
    rank_ref[0] = pack(ranks).astype(jnp.int32)


def _outproj_call(cnt0, parts, w_out_bf16, x, gt, g, sc, sh, w_router, b_router, *, tm):
    B, L, D = x.shape
    per_batch = gt.shape[0] > 1
    mod_map = (lambda b, i: (b, 0, 0)) if per_batch else (lambda b, i: (0, 0, 0))
    part_spec = pl.BlockSpec((1, tm, GROUP_WIDTH), lambda b, i: (b, i, 0))
    row_spec = pl.BlockSpec((1, tm, D), lambda b, i: (b, i, 0))
    meta_spec = pl.BlockSpec((1, tm, 128), lambda b, i: (b, i, 0))
    cnt_spec = pl.BlockSpec((1, N_EXPERTS), lambda b, i: (0, 0))
    return pl.pallas_call(
        _outproj_kernel,
        grid=(B, L // tm),
        in_specs=[cnt_spec, part_spec, part_spec, part_spec, part_spec,
                  pl.BlockSpec((D, D), lambda b, i: (0, 0)),
                  row_spec,
                  pl.BlockSpec((1, 1, D), mod_map),
                  pl.BlockSpec((1, D), lambda b, i: (0, 0)),
                  pl.BlockSpec((1, 1, D), mod_map),
                  pl.BlockSpec((1, 1, D), mod_map),
                  pl.BlockSpec((D, N_EXPERTS), lambda b, i: (0, 0)),
                  cnt_spec],
        out_specs=[row_spec, row_spec, meta_spec, meta_spec, meta_spec, cnt_spec],
        out_shape=[jax.ShapeDtypeStruct((B, L, D), F32),
                   jax.ShapeDtypeStruct((B, L, D), F32),
                   jax.ShapeDtypeStruct((B, L, 128), jnp.int32),
                   jax.ShapeDtypeStruct((B, L, 128), F32),
                   jax.ShapeDtypeStruct((B, L, 128), jnp.int32),
                   jax.ShapeDtypeStruct((1, N_EXPERTS), F32)],
        scratch_shapes=[pltpu.VMEM((1, N_EXPERTS), F32)],
        compiler_params=_cparams(("arbitrary", "arbitrary")),
        name="out_proj_router",
    )(cnt0, *parts, w_out_bf16, x, gt, g.reshape(1, D), sc, sh, w_router,
      b_router.reshape(1, N_EXPERTS))


def _dispatch_kernel(dest_ref, hf_ref, xs_in, xs_out, sem, *, tm):
    del xs_in

    def body(i, carry):
        for k in range(TOP_K):
            d = dest_ref[0, 0, i * TOP_K + k]
            pltpu.make_async_copy(hf_ref.at[pl.ds(i, 1)], xs_out.at[pl.ds(d, 1)], sem).start()
        return carry

    lax.fori_loop(0, tm, body, 0)
    for k in range(TOP_K):
        pltpu.make_async_copy(hf_ref, xs_out.at[pl.ds(0, tm)], sem).wait()


def _dispatch_call(dest, hf2d, xs, *, tm):
    T, D = hf2d.shape
    n_tiles = T // tm
    return pl.pallas_call(
        functools.partial(_dispatch_kernel, tm=tm),
        grid=(n_tiles,),
        in_specs=[pl.BlockSpec((1, 1, tm * TOP_K), lambda i: (i, 0, 0), memory_space=pltpu.SMEM),
                  pl.BlockSpec((tm, D), lambda i: (i, 0)),
                  pl.BlockSpec(memory_space=pl.ANY)],
        out_specs=pl.BlockSpec(memory_space=pl.ANY),
        out_shape=jax.ShapeDtypeStruct(xs.shape, xs.dtype),
        scratch_shapes=[pltpu.SemaphoreType.DMA(())],
        input_output_aliases={2: 0},
        compiler_params=_cparams(("arbitrary",)),
        name="moe_dispatch",
    )(dest.reshape(n_tiles, 1, tm * TOP_K), hf2d, xs)


def _expert_kernel(be_ref, na_ref, xs_ref, wgu_ref, bgu_ref, wd_ref, bd_ref, y_ref):
    del be_ref

    @pl.when(pl.program_id(0) < na_ref[0])
    def _():
        xb = xs_ref[...].astype(BF16)
        gu = jnp.dot(xb, wgu_ref[0], preferred_element_type=F32) + bgu_ref[0]
        gt = jnp.minimum(gu[:, :D_EXPERT], SWIGLU_LIMIT)
        up = jnp.clip(gu[:, D_EXPERT:], -SWIGLU_LIMIT, SWIGLU_LIMIT)
        act = (up + 1.0) * gt * (1.0 / (1.0 + jnp.exp(-SWIGLU_ALPHA * gt)))
        y_ref[...] = jnp.dot(act.astype(BF16), wd_ref[0], preferred_element_type=F32) + bd_ref[0]

    @pl.when(pl.program_id(0) >= na_ref[0])
    def _():
        y_ref[...] = jnp.zeros(y_ref.shape, y_ref.dtype)


def _expert_call(blk_e, n_active, xs, wgu, bgu, wd, bd):
    n_slots, D = xs.shape
    nb = n_slots // MOE_BM

    def row_map(i, be, na):
        return (jnp.minimum(i, jnp.maximum(na[0] - 1, 0)), 0)

    def w_map(i, be, na):
        return (be[jnp.minimum(i, jnp.maximum(na[0] - 1, 0))], 0, 0)

    return pl.pallas_call(
        _expert_kernel,
        grid_spec=pltpu.PrefetchScalarGridSpec(
            num_scalar_prefetch=2,
            grid=(nb,),
            in_specs=[pl.BlockSpec((MOE_BM, D), row_map),
                      pl.BlockSpec((1, D, 2 * D_EXPERT), w_map),
                      pl.BlockSpec((1, 1, 2 * D_EXPERT), w_map),
                      pl.BlockSpec((1, D_EXPERT, D), w_map),
                      pl.BlockSpec((1, 1, D), w_map)],
            out_specs=pl.BlockSpec((MOE_BM, D), lambda i, be, na: (i, 0))),
        out_shape=jax.ShapeDtypeStruct((n_slots, D), F32),
        compiler_params=_cparams(("arbitrary",)),
        name="moe_experts",
    )(blk_e, n_active, xs, wgu, bgu.reshape(N_EXPERTS, 1, 2 * D_EXPERT), wd,
      bd.reshape(N_EXPERTS, 1, D))


def _combine_kernel(dest_ref, gate_ref, x_ref, gt_ref, gfin_ref, yb_ref, o_ref, buf, sem, *,
                    tm, final_norm):
    def body(i, carry):
        for k in range(TOP_K):
            d = dest_ref[0, 0, i * TOP_K + k]
            pltpu.make_async_copy(yb_ref.at[pl.ds(d, 1)], buf.at[k, pl.ds(i, 1)], sem).start()
        return carry

    lax.fori_loop(0, tm, body, 0)
    for k in range(TOP_K):
        pltpu.make_async_copy(yb_ref.at[pl.ds(0, tm)], buf.at[k], sem).wait()
    gates = gate_ref[0]
    y = buf[0] * gates[:, 0:1]
    for k in range(1, TOP_K):
        y = y + buf[k] * gates[:, k:k + 1]
    x = x_ref[0] + gt_ref[0] * y
    if final_norm:
        ms = jnp.mean(x * x, axis=-1, keepdims=True)
        x = x * lax.rsqrt(ms + EPS) * gfin_ref[...]
    o_ref[0] = x


def _combine_call(dest, gates, x, gt, g_final, yb, *, tm, final_norm):
    B, L, D = x.shape
    n_l = L // tm
    per_batch = gt.shape[0] > 1
    mod_map = (lambda b, i: (b, 0, 0)) if per_batch else (lambda b, i: (0, 0, 0))
    return pl.pallas_call(
        functools.partial(_combine_kernel, tm=tm, final_norm=final_norm),
        grid=(B, n_l),
        in_specs=[pl.BlockSpec((1, 1, tm * TOP_K), lambda b, i: (b * n_l + i, 0, 0),
                               memory_space=pltpu.SMEM),
                  pl.BlockSpec((1, tm, 128), lambda b, i: (b, i, 0)),
                  pl.BlockSpec((1, tm, D), lambda b, i: (b, i, 0)),
                  pl.BlockSpec((1, 1, D), mod_map),
                  pl.BlockSpec((1, D), lambda b, i: (0, 0)),
                  pl.BlockSpec(memory_space=pl.ANY)],
        out_specs=pl.BlockSpec((1, tm, D), lambda b, i: (b, i, 0)),
        out_shape=jax.ShapeDtypeStruct((B, L, D), F32),
        scratch_shapes=[pltpu.VMEM((TOP_K, tm, D), F32), pltpu.SemaphoreType.DMA(())],
        compiler_params=_cparams(("arbitrary", "arbitrary")),
        name="moe_combine",
    )(dest.reshape(B * n_l, 1, tm * TOP_K), gates, x, gt, g_final.reshape(1, D), yb)


def _moe(streams, p_bf16, b_gu, b_down, g_final, final_norm):
    counts = streams[-1]["cnt"].reshape(N_EXPERTS).astype(jnp.int32)
    padded = (counts + MOE_BM - 1) // MOE_BM * MOE_BM
    cum_pad = jnp.cumsum(padded)
    pstart = cum_pad - padded
    total_tokens = sum(s["x"].shape[0] * s["x"].shape[1] for s in streams)
    nb = -(-total_tokens * TOP_K // MOE_BM) + N_EXPERTS
    blk_e = jnp.minimum(jnp.sum(cum_pad[None, :] <= jnp.arange(nb)[:, None] * MOE_BM, axis=1),
                        N_EXPERTS - 1).astype(jnp.int32)
    expert_ids = jnp.arange(N_EXPERTS, dtype=jnp.int32)
    n_active = (cum_pad[-1] // MOE_BM).astype(jnp.int32).reshape(1)

    xs = jnp.zeros((nb * MOE_BM, D_MODEL), F32)
    dests = []
    for s in streams:
        B, L, _ = s["x"].shape
        idx = s["idx"][..., :TOP_K]
        seg = jnp.sum(jnp.where(idx[..., None] == expert_ids, pstart, 0), axis=-1)
        dest = (seg + s["rank"][..., :TOP_K]).reshape(B * L * TOP_K)
        dests.append(dest)
        xs = _dispatch_call(dest, s["hf"].reshape(B * L, D_MODEL), xs, tm=min(256, L))
    yb = _expert_call(blk_e, n_active, xs, p_bf16["w_gu"], b_gu, p_bf16["w_down"], b_down)
    outs = []
    for s, dest in zip(streams, dests):
        L = s["x"].shape[1]
        outs.append(_combine_call(dest, s["gate"], s["x"], s["gt"], g_final, yb,
                                  tm=min(128, L), final_norm=final_norm))
    return outs


def _layer(x, xc, c8, p, layer_idx, update_ctx, tabs, g_final, final_norm):
    B, S, D = x.shape
    C = xc.shape[1]
    lam_init = 0.8 - 0.6 * math.exp(-0.3 * layer_idx)
    lam = (jnp.exp(jnp.sum(p["lam_q1"] * p["lam_k1"])) - jnp.exp(jnp.sum(p["lam_q2"] * p["lam_k2"]))
           + lam_init).reshape(1).astype(F32)

    mod8 = _mod_call(c8, p["w_mod"], p["b_mod"])
    mod = [m.reshape(B, 1, D) for m in jnp.split(mod8[:B], 6, axis=-1)]
    modc = [m.reshape(1, 1, D) for m in jnp.split(mod8[B:B + 1], 6, axis=-1)]
    sh1, sc1, gt1, sh2, sc2, gt2 = mod

    w_in = p["w_in"].astype(BF16)
    w_out = p["w_out"].astype(BF16)
    proj = _inproj_call(x, sc1, sh1, p["g_mix"], w_in, tabs["cos"], tabs["sin"], rope=True, tm=512)
    projc = _inproj_call(xc, modc[1], modc[0], p["g_mix"], w_in, tabs["cos"], tabs["sin"],
                         rope=False, tm=C)

    gsub_col = p["g_sub"].reshape(HEAD_DIM, 1)
    oa = _da_call(lam, proj, proj, projc, gsub_col, lam_init=lam_init, tq=256, tkc=512)
    bias = _na_bias_tables(p["rpb"], S // GRID_W)
    od = _na_call(proj, projc, bias)

    wpool_bd = jax.scipy.linalg.block_diag(*[p["w_pool"][g] for g in range(4)]).astype(BF16)
    spool = p["s_pool"].reshape(1, GROUP_WIDTH)
    gsgu = p["g_sgu"].reshape(1, GROUP_WIDTH)
    wsgu_cat = jnp.transpose(p["w_sgu"], (1, 0, 2)).reshape(SGU_CHUNK, N_HEADS * SGU_CHUNK).astype(BF16)
    bsgu_full = jnp.repeat(p["b_sgu"].T, HEAD_DIM, axis=1)
    pool, sgu = _mixers_call(proj, wpool_bd, spool, gsgu, wsgu_cat, bsgu_full, tm=512)

    cnt0 = jnp.zeros((1, N_EXPERTS), F32)
    x1, hf, idx, gate, rank, cnt = _outproj_call(
        cnt0, (oa, pool, sgu, od), w_out, x, gt1, p["g_ffn"], sc2, sh2,
        p["w_router"], p["b_router"], tm=256)
    streams = [dict(x=x1, hf=hf, idx=idx, gate=gate, rank=rank, gt=gt2, cnt=cnt)]

    if update_ctx:
        oa_c = _da_call(lam, projc, None, projc, gsub_col, lam_init=lam_init,
                        tq=C, tkc=512)
        od_c = _ctx_attn_call(projc)
        pool_c, sgu_c = _mixers_call(projc, wpool_bd, spool, gsgu, wsgu_cat, bsgu_full, tm=C)
        xc1, hfc, idxc, gatec, rankc, cnt = _outproj_call(
            cnt, (oa_c, pool_c, sgu_c, od_c), w_out, xc, modc[2], p["g_ffn"], modc[4], modc[3],
            p["w_router"], p["b_router"], tm=C)
        streams.append(dict(x=xc1, hf=hfc, idx=idxc, gate=gatec, rank=rankc, gt=modc[5], cnt=cnt))

    p_bf16 = dict(w_gu=p["w_gu"].astype(BF16), w_down=p["w_down"].astype(BF16))
    outs = _moe(streams, p_bf16, p["b_gu"], p["b_down"], g_final, final_norm)
    if update_ctx:
        return outs[0], outs[1]
    return outs[0], xc


def kernel(x, c, ctx, c_ctx, w_mod, b_mod, g_mix, g_ffn, w_in, w_out, lam_q1, lam_k1, lam_q2,
           lam_k2, g_sub, w_pool, s_pool, g_sgu, w_sgu, b_sgu, rpb, w_router, b_router, w_gu,
           b_gu, w_down, b_down, g_final):
    B, S, D = x.shape
    cos_tab, sin_tab = _rope_tables(S)
    tabs = dict(cos=cos_tab, sin=sin_tab)
    c8 = jnp.concatenate([c, c_ctx[None, :], jnp.zeros((8 - B - 1, D), F32)], axis=0)
    xc = ctx
    for l in range(DEPTH):
        p = dict(w_mod=w_mod[l], b_mod=b_mod[l], g_mix=g_mix[l], g_ffn=g_ffn[l], w_in=w_in[l],
                 w_out=w_out[l], lam_q1=lam_q1[l], lam_k1=lam_k1[l], lam_q2=lam_q2[l],
                 lam_k2=lam_k2[l], g_sub=g_sub[l], w_pool=w_pool[l], s_pool=s_pool[l],
                 g_sgu=g_sgu[l], w_sgu=w_sgu[l], b_sgu=b_sgu[l], rpb=rpb[l],
                 w_router=w_router[l], b_router=b_router[l], w_gu=w_gu[l], b_gu=b_gu[l],
                 w_down=w_down[l], b_down=b_down[l])
        x, xc = _layer(x, xc, c8, p, l, l < DEPTH - 1, tabs, g_final, l == DEPTH - 1)
    return x
```

```python
import functools
import math

import jax
import jax.numpy as jnp
import numpy as np
from jax import lax
from jax.experimental import pallas as pl
from jax.experimental.pallas import tpu as pltpu

F32 = jnp.float32
BF16 = jnp.bfloat16

D_MODEL = 1024
DEPTH = 2
GRID_W = 64
HEAD_DIM = 64
N_HEADS = 4
GROUP_WIDTH = N_HEADS * HEAD_DIM
PROJ_WIDTH = 9 * GROUP_WIDTH
DA_QK = HEAD_DIM // 2
ROPE_BASE = 10000.0
POOL_WINDOWS = (2, 4, 8, 16)
SGU_CHUNK = 128
WIN_R = 8
WIN_C = 16
N_EXPERTS = 32
TOP_K = 4
D_EXPERT = D_MODEL
SWIGLU_LIMIT = 7.0
SWIGLU_ALPHA = 1.702
EPS = 1e-6
NEG_INF = -1e30
LOG2_E = math.log2(math.e)

COL_AQ, COL_AK, COL_AV, COL_POOL, COL_SGU, COL_NQ, COL_NK, COL_NV = 0, 1, 2, 3, 4, 6, 7, 8

NA_QROWS = 4
NA_BAND = 12
POOL_HALO = 16
MOE_BM = 256
VMEM_LIMIT = 48 * 1024 * 1024


def _cparams(sem):
    return pltpu.CompilerParams(dimension_semantics=sem, vmem_limit_bytes=VMEM_LIMIT)


def _lane_iota(n):
    return lax.broadcasted_iota(jnp.int32, (1, n), 1)


def _dot_nt(a, b):
    return lax.dot_general(a, b, (((1,), (1,)), ((), ())), preferred_element_type=F32)


def _mod_kernel(c_ref, w_ref, b_ref, o_ref):
    c = c_ref[...]
    s = c * (1.0 / (1.0 + jnp.exp(-c)))
    o_ref[...] = jnp.dot(s.astype(BF16), w_ref[...].astype(BF16),
                         preferred_element_type=F32) + b_ref[...]


def _mod_call(c8, w_mod, b_mod):
    n = w_mod.shape[1]
    tn = 1536
    return pl.pallas_call(
        _mod_kernel,
        grid=(n // tn,),
        in_specs=[pl.BlockSpec((8, D_MODEL), lambda j: (0, 0)),
                  pl.BlockSpec((D_MODEL, tn), lambda j: (0, j)),
                  pl.BlockSpec((1, tn), lambda j: (0, j))],
        out_specs=pl.BlockSpec((8, tn), lambda j: (0, j)),
        out_shape=jax.ShapeDtypeStruct((8, n), F32),
        compiler_params=_cparams(("arbitrary",)),
        name="adaln_mod",
    )(c8, w_mod, b_mod.reshape(1, n))


def _swap8(x):
    n = x.shape[-1]
    lane = _lane_iota(n)
    first_half = (lane & 15) < 8
    return jnp.where(first_half, pltpu.roll(x, n - 8, axis=1), pltpu.roll(x, 8, axis=1))


def _inproj_kernel(x_ref, sc_ref, sh_ref, g_ref, w_ref, cos_ref, sin_ref, o_ref, *, rope):
    x = x_ref[0]
    ms = jnp.mean(x * x, axis=-1, keepdims=True)
    y = x * lax.rsqrt(ms + EPS) * g_ref[...]
    h = (y * (1.0 + sc_ref[0]) + sh_ref[0]).astype(BF16)
    for j in range(PROJ_WIDTH // GROUP_WIDTH):
        cols = slice(j * GROUP_WIDTH, (j + 1) * GROUP_WIDTH)
        acc = jnp.dot(h, w_ref[:, cols], preferred_element_type=F32)
        if rope and j in (COL_AQ, COL_AK):
            acc = acc * cos_ref[...] + _swap8(acc) * sin_ref[...]
        if j == COL_AQ:
            acc = acc * (DA_QK ** -0.5 * LOG2_E)
        if j == COL_NQ:
            acc = acc * (HEAD_DIM ** -0.5)
        o_ref[0, :, cols] = acc.astype(o_ref.dtype)


def _inproj_call(x, sc, sh, g, w_bf16, cos_tab, sin_tab, *, rope, tm):
    B, L, D = x.shape
    per_batch = sc.shape[0] > 1
    mod_map = (lambda b, i: (b, 0, 0)) if per_batch else (lambda b, i: (0, 0, 0))
    tab_map = (lambda b, i: (i, 0)) if rope else (lambda b, i: (0, 0))
    return pl.pallas_call(
        functools.partial(_inproj_kernel, rope=rope),
        grid=(B, L // tm),
        in_specs=[pl.BlockSpec((1, tm, D), lambda b, i: (b, i, 0)),
                  pl.BlockSpec((1, 1, D), mod_map),
                  pl.BlockSpec((1, 1, D), mod_map),
                  pl.BlockSpec((1, D), lambda b, i: (0, 0)),
                  pl.BlockSpec((D, PROJ_WIDTH), lambda b, i: (0, 0)),
                  pl.BlockSpec((tm, GROUP_WIDTH), tab_map),
                  pl.BlockSpec((tm, GROUP_WIDTH), tab_map)],
        out_specs=pl.BlockSpec((1, tm, PROJ_WIDTH), lambda b, i: (b, i, 0)),
        out_shape=jax.ShapeDtypeStruct((B, L, PROJ_WIDTH), BF16),
        compiler_params=_cparams(("arbitrary", "arbitrary")),
        name="in_proj",
    )(x, sc, sh, g.reshape(1, D), w_bf16, cos_tab, sin_tab)


def _rope_tables(S):
    n = 8
    inv = ROPE_BASE ** (-jnp.arange(n, dtype=F32) / n)
    t = jnp.arange(S)
    row_pos = (t // GRID_W).astype(F32)
    col_pos = (t % GRID_W).astype(F32)
    parts_c, parts_s = [], []
    for pos in (row_pos, col_pos):
        ang = pos[:, None] * inv[None, :]
        c, s = jnp.cos(ang), jnp.sin(ang)
        parts_c += [c, c]
        parts_s += [-s, s]
    cos32 = jnp.concatenate(parts_c, axis=1)
    sin32 = jnp.concatenate(parts_s, axis=1)
    reps = GROUP_WIDTH // 32
    return jnp.tile(cos32, (1, reps)), jnp.tile(sin32, (1, reps))


def _da_kernel(lam_ref, *refs, n_lat, tkc, post_scale):
    if n_lat:
        q_ref, kl_ref, vlt_ref, kc_ref, vct_ref, gsub_ref, o_ref = refs[:7]
    else:
        q_ref, kc_ref, vct_ref, gsub_ref, o_ref = refs[:5]
    qs_scr, m_scr, l_scr, acc_scr = refs[-4:]
    q = q_ref[0]
    tq = q.shape[0]
    lane = _lane_iota(GROUP_WIDTH)
    for g in range(2 * N_HEADS):
        sel = (lane >= g * DA_QK) & (lane < (g + 1) * DA_QK)
        qs_scr[g * tq:(g + 1) * tq, :] = q * sel.astype(BF16)
    m_scr[...] = jnp.full(m_scr.shape, NEG_INF, F32)
    l_scr[...] = jnp.zeros(l_scr.shape, F32)
    acc_scr[...] = jnp.zeros(acc_scr.shape, F32)

    def step(k, vt):
        st = _dot_nt(k, qs_scr[...])
        m_old = m_scr[...]
        m_new = jnp.maximum(m_old, jnp.max(st, axis=0, keepdims=True))
        alpha = jnp.exp2(m_old - m_new)
        p = jnp.exp2(st - m_new)
        l_scr[...] = alpha * l_scr[...] + jnp.sum(p, axis=0, keepdims=True)
        m_scr[...] = m_new
        pb = p.astype(BF16)
        for h in range(N_HEADS):
            rows = slice(h * HEAD_DIM, (h + 1) * HEAD_DIM)
            cols = slice(2 * h * tq, (2 * h + 2) * tq)
            pv = jnp.dot(vt[rows, :], pb[:, cols], preferred_element_type=F32)
            acc_scr[rows, :] = alpha[:, cols] * acc_scr[rows, :] + pv

    if n_lat:
        def body(c, carry):
            off = pl.multiple_of(c * tkc, tkc)
            step(kl_ref[0, pl.ds(off, tkc), :], vlt_ref[0, c])
            return carry
        lax.fori_loop(0, n_lat, body, 0)
    step(kc_ref[0], vct_ref[0, 0])

    lam = lam_ref[0]
    outs = []
    for h in range(N_HEADS):
        rows = slice(h * HEAD_DIM, (h + 1) * HEAD_DIM)
        cols = slice(2 * h * tq, (2 * h + 2) * tq)
        o = acc_scr[rows, :] / l_scr[:, cols]
        o_h = o[:, :tq] - lam * o[:, tq:]
        ms = jnp.mean(o_h * o_h, axis=0, keepdims=True)
        outs.append(o_h * lax.rsqrt(ms + EPS) * gsub_ref[...] * post_scale)
    o_ref[0] = jnp.concatenate(outs, axis=0).T.astype(o_ref.dtype)


def _values_t(proj, tkc):
    B, L, _ = proj.shape
    v = proj[:, :, COL_AV * GROUP_WIDTH:(COL_AV + 1) * GROUP_WIDTH]
    return jnp.transpose(v.reshape(B, L // tkc, tkc, GROUP_WIDTH), (0, 1, 3, 2))


def _da_call(lam, proj_q, proj_lat, proj_ctx, gsub_col, *, lam_init, tq, tkc):
    B, Lq, _ = proj_q.shape
    C = proj_ctx.shape[1]
    in_specs = [pl.BlockSpec(memory_space=pltpu.SMEM),
                pl.BlockSpec((1, tq, GROUP_WIDTH), lambda b, i: (b, i, COL_AQ))]
    args = [lam, proj_q]
    n_lat = 0
    if proj_lat is not None:
        S = proj_lat.shape[1]
        n_lat = S // tkc
        in_specs += [pl.BlockSpec((1, S, GROUP_WIDTH), lambda b, i: (b, 0, COL_AK)),
                     pl.BlockSpec((1, n_lat, GROUP_WIDTH, tkc), lambda b, i: (b, 0, 0, 0))]
        args += [proj_lat, _values_t(proj_lat, tkc)]
    in_specs += [pl.BlockSpec((1, C, GROUP_WIDTH), lambda b, i: (b, 0, COL_AK)),
                 pl.BlockSpec((1, 1, GROUP_WIDTH, C), lambda b, i: (b, 0, 0, 0)),
                 pl.BlockSpec((HEAD_DIM, 1), lambda b, i: (0, 0))]
    args += [proj_ctx, _values_t(proj_ctx, C), gsub_col]
    n_stack = 2 * N_HEADS * tq
    return pl.pallas_call(
        functools.partial(_da_kernel, n_lat=n_lat, tkc=tkc, post_scale=1.0 - lam_init),
        grid=(B, Lq // tq),
        in_specs=in_specs,
        out_specs=pl.BlockSpec((1, tq, GROUP_WIDTH), lambda b, i: (b, i, 0)),
        out_shape=jax.ShapeDtypeStruct((B, Lq, GROUP_WIDTH), BF16),
        scratch_shapes=[pltpu.VMEM((n_stack, GROUP_WIDTH), BF16),
                        pltpu.VMEM((1, n_stack), F32),
                        pltpu.VMEM((1, n_stack), F32),
                        pltpu.VMEM((GROUP_WIDTH, 2 * tq), F32)],
        compiler_params=_cparams(("arbitrary", "arbitrary")),
        name="diff_attn" if n_lat else "diff_attn_ctx",
    )(*args)


def _softmax_heads(q, sources, out_dtype):
    tq = q.shape[0]
    lane = _lane_iota(GROUP_WIDTH)
    out = jnp.zeros((tq, GROUP_WIDTH), F32)
    for h in range(N_HEADS):
        hm = (lane >= h * HEAD_DIM) & (lane < (h + 1) * HEAD_DIM)
        qm = q * hm.astype(BF16)
        scores = []
        for k, _, bias_fn in sources:
            s = _dot_nt(qm, k)
            if bias_fn is not None:
                s = s + bias_fn(h)
            scores.append(s)
        m = scores[0].max(axis=-1, keepdims=True)
        for s in scores[1:]:
            m = jnp.maximum(m, s.max(axis=-1, keepdims=True))
        l = jnp.zeros((tq, 1), F32)
        o = jnp.zeros((tq, GROUP_WIDTH), F32)
        for s, (_, v, _) in zip(scores, sources):
            p = jnp.exp(s - m)
            l = l + jnp.sum(p, axis=-1, keepdims=True)
            o = o + jnp.dot(p.astype(BF16), v, preferred_element_type=F32)
        out = jnp.where(hm, o / l, out)
    return out.astype(out_dtype)


def _na_kernel(q_ref, kn_ref, vn_ref, kc_ref, vc_ref, bias_ref, o_ref, *, n_groups):
    g = pl.program_id(1)
    tq = NA_QROWS * GRID_W
    band = NA_BAND * GRID_W
    start = pl.multiple_of(jnp.clip(g - 1, 0, n_groups - NA_BAND // NA_QROWS) * tq, tq)
    kb = kn_ref[0, pl.ds(start, band), :]
    vb = vn_ref[0, pl.ds(start, band), :]
    sources = [(kb, vb, lambda h: bias_ref[0, h]), (kc_ref[0], vc_ref[0], None)]
    o_ref[0] = _softmax_heads(q_ref[0], sources, o_ref.dtype)


def _na_call(proj, proj_ctx, bias):
    B, S, _ = proj.shape
    C = proj_ctx.shape[1]
    tq = NA_QROWS * GRID_W
    band = NA_BAND * GRID_W
    n_groups = S // tq

    def bias_map(b, g):
        return (jnp.where(g == 0, 0, jnp.where(g == n_groups - 1, 2, 1)), 0, 0, 0)

    return pl.pallas_call(
        functools.partial(_na_kernel, n_groups=n_groups),
        grid=(B, n_groups),
        in_specs=[pl.BlockSpec((1, tq, GROUP_WIDTH), lambda b, g: (b, g, COL_NQ)),
                  pl.BlockSpec((1, S, GROUP_WIDTH), lambda b, g: (b, 0, COL_NK)),
                  pl.BlockSpec((1, S, GROUP_WIDTH), lambda b, g: (b, 0, COL_NV)),
                  pl.BlockSpec((1, C, GROUP_WIDTH), lambda b, g: (b, 0, COL_NK)),
                  pl.BlockSpec((1, C, GROUP_WIDTH), lambda b, g: (b, 0, COL_NV)),
                  pl.BlockSpec((1, N_HEADS, tq, band), bias_map)],
        out_specs=pl.BlockSpec((1, tq, GROUP_WIDTH), lambda b, g: (b, g, 0)),
        out_shape=jax.ShapeDtypeStruct((B, S, GROUP_WIDTH), BF16),
        compiler_params=_cparams(("arbitrary", "arbitrary")),
        name="nbr_attn",
    )(proj, proj, proj, proj_ctx, proj_ctx, bias)


def _na_bias_tables(rpb, rows):
    n_groups = rows // NA_QROWS
    n_rr, n_rc = 2 * WIN_R - 1, 2 * WIN_C - 1
    c = np.arange(GRID_W)[:, None]
    kc = np.arange(GRID_W)[None, :]
    col_lo = np.clip(c - WIN_C // 2, 0, GRID_W - WIN_C)
    col_ok = (kc >= col_lo) & (kc < col_lo + WIN_C)
    rel_c = np.clip(kc - c, -(WIN_C - 1), WIN_C - 1) + (WIN_C - 1)
    sel_c = (rel_c[..., None] == np.arange(n_rc)).astype(np.float32)
    sel_r, ok = [], []
    for g in (0, 1, n_groups - 1):
        r0 = g * NA_QROWS
        start = min(max(r0 - WIN_R // 2, 0), rows - NA_BAND)
        r = r0 + np.arange(NA_QROWS)[:, None]
        kr = start + np.arange(NA_BAND)[None, :]
        row_lo = np.clip(r - WIN_R // 2, 0, rows - WIN_R)
        row_ok = (kr >= row_lo) & (kr < row_lo + WIN_R)
        rel_r = kr - r + (WIN_R - 1)
        sel_r.append((rel_r[..., None] == np.arange(n_rr)).astype(np.float32))
        ok.append(row_ok[:, None, :, None] & col_ok[None, :, None, :])
    sel_r = jnp.asarray(np.stack(sel_r))
    ok = np.stack(ok).reshape(3, 1, NA_QROWS * GRID_W, NA_BAND * GRID_W)
    vals = jnp.einsum("vxka,hab,cyb->vhxcky", sel_r, rpb.astype(F32), jnp.asarray(sel_c),
                      precision=lax.Precision.HIGHEST)
    vals = vals.reshape(3, N_HEADS, NA_QROWS * GRID_W, NA_BAND * GRID_W)
    return jnp.where(jnp.asarray(ok), vals, NEG_INF)


def _ctx_attn_kernel(q_ref, k_ref, v_ref, o_ref):
    o_ref[0] = _softmax_heads(q_ref[0], [(k_ref[0], v_ref[0], None)], o_ref.dtype)


def _ctx_attn_call(proj_ctx):
    B, C, _ = proj_ctx.shape
    return pl.pallas_call(
        _ctx_attn_kernel,
        grid=(B,),
        in_specs=[pl.BlockSpec((1, C, GROUP_WIDTH), lambda b: (b, 0, COL_NQ)),
                  pl.BlockSpec((1, C, GROUP_WIDTH), lambda b: (b, 0, COL_NK)),
                  pl.BlockSpec((1, C, GROUP_WIDTH), lambda b: (b, 0, COL_NV))],
        out_specs=pl.BlockSpec((1, C, GROUP_WIDTH), lambda b: (b, 0, 0)),
        out_shape=jax.ShapeDtypeStruct((B, C, GROUP_WIDTH), BF16),
        compiler_params=_cparams(("arbitrary",)),
        name="ctx_attn",
    )(proj_ctx, proj_ctx, proj_ctx)


def _gelu_tanh(x):
    return 0.5 * x * (1.0 + jnp.tanh(math.sqrt(2.0 / math.pi) * (x + 0.044715 * (x * x * x))))


def _mixers_kernel(pc_ref, pp_ref, pn_ref, sg_ref, wpool_ref, spool_ref, gsgu_ref, wsgu_ref,
                   bsgu_ref, pool_o, sgu_o, *, tm, seq_len):
    i = pl.program_id(1)
    n_tiles = seq_len // tm
    lane = _lane_iota(GROUP_WIDTH)
    grp = lax.shift_right_logical(lane, 6)

    def by_group(vals):
        return jnp.where(grp == 0, vals[0], jnp.where(grp == 1, vals[1],
                                                       jnp.where(grp == 2, vals[2], vals[3])))

    cur = pc_ref[0].astype(F32)
    prev = jnp.where(i > 0, pp_ref[0].astype(F32), 0.0)
    nxt = jnp.where(i < n_tiles - 1, pn_ref[0].astype(F32), 0.0)
    ext = jnp.concatenate([prev, cur, nxt], axis=0)
    n = tm + 2 * POOL_HALO

    def ahead(x, d):
        return pltpu.roll(x, n - d, axis=0)

    def behind(x, d):
        return pltpu.roll(x, d, axis=0)

    f2 = ext + ahead(ext, 1)
    f4 = f2 + ahead(f2, 2)
    f8 = f4 + ahead(f4, 4)
    f16 = f8 + ahead(f8, 8)
    sums = by_group([behind(f2, 1), behind(f4, 2), behind(f8, 4), behind(f16, 8)])
    sums = sums[POOL_HALO:POOL_HALO + tm]
    t = i * tm + lax.broadcasted_iota(jnp.int32, (tm, 1), 0)
    cnts = []
    for w in POOL_WINDOWS:
        lo = jnp.maximum(t - w // 2, 0)
        hi = jnp.minimum(t + (w - w // 2 - 1), seq_len - 1)
        cnts.append((hi - lo + 1).astype(F32))
    cnt = by_group(cnts)
    diff = (sums / cnt - cur).astype(BF16)
    pool = jnp.dot(diff, wpool_ref[...], preferred_element_type=F32) * spool_ref[...]
    pool_o[0] = pool.astype(pool_o.dtype)

    z = _gelu_tanh(sg_ref[0].astype(F32))
    u = z[:, :GROUP_WIDTH]
    v = z[:, GROUP_WIDTH:]
    v = v * lax.rsqrt(jnp.mean(v * v, axis=-1, keepdims=True) + EPS) * gsgu_ref[...]
    v = v.astype(BF16)
    hmasks = [(grp == h).astype(BF16) for h in range(N_HEADS)]
    for c in range(tm // SGU_CHUNK):
        rows = slice(c * SGU_CHUNK, (c + 1) * SGU_CHUNK)
        vc = v[rows]
        vstack = jnp.concatenate([vc * hm for hm in hmasks], axis=0)
        s = jnp.dot(wsgu_ref[...], vstack, preferred_element_type=F32) + bsgu_ref[...]
        sgu_o[0, rows, :] = (u[rows] * s).astype(sgu_o.dtype)


def _mixers_call(proj, wpool_bd, spool, gsgu, wsgu_cat, bsgu_full, *, tm):
    B, L, _ = proj.shape
    hb = tm // POOL_HALO
    n_halo = L // POOL_HALO
    return pl.pallas_call(
        functools.partial(_mixers_kernel, tm=tm, seq_len=L),
        grid=(B, L // tm),
        in_specs=[pl.BlockSpec((1, tm, GROUP_WIDTH), lambda b, i: (b, i, COL_POOL)),
                  pl.BlockSpec((1, POOL_HALO, GROUP_WIDTH),
                               lambda b, i: (b, jnp.maximum(i * hb - 1, 0), COL_POOL)),
                  pl.BlockSpec((1, POOL_HALO, GROUP_WIDTH),
                               lambda b, i: (b, jnp.minimum((i + 1) * hb, n_halo - 1), COL_POOL)),
                  pl.BlockSpec((1, tm, 2 * GROUP_WIDTH), lambda b, i: (b, i, COL_SGU // 2)),
                  pl.BlockSpec((GROUP_WIDTH, GROUP_WIDTH), lambda b, i: (0, 0)),
                  pl.BlockSpec((1, GROUP_WIDTH), lambda b, i: (0, 0)),
                  pl.BlockSpec((1, GROUP_WIDTH), lambda b, i: (0, 0)),
                  pl.BlockSpec((SGU_CHUNK, N_HEADS * SGU_CHUNK), lambda b, i: (0, 0)),
                  pl.BlockSpec((SGU_CHUNK, GROUP_WIDTH), lambda b, i: (0, 0))],
        out_specs=[pl.BlockSpec((1, tm, GROUP_WIDTH), lambda b, i: (b, i, 0)),
                   pl.BlockSpec((1, tm, GROUP_WIDTH), lambda b, i: (b, i, 0))],
        out_shape=[jax.ShapeDtypeStruct((B, L, GROUP_WIDTH), BF16),
                   jax.ShapeDtypeStruct((B, L, GROUP_WIDTH), BF16)],
        compiler_params=_cparams(("arbitrary", "arbitrary")),
        name="pool_sgu",
    )(proj, proj, proj, proj, wpool_bd, spool, gsgu, wsgu_cat, bsgu_full)


def _outproj_kernel(cnt0_ref, p0, p1, p2, p3, w_ref, x_ref, gt_ref, g_ref, sc_ref, sh_ref,
                    wr3_ref, br_ref, xo_ref, hf_ref, idx_ref, gate_ref, rank_ref, cnt_ref, cnt_scr):
    first = (pl.program_id(0) == 0) & (pl.program_id(1) == 0)

    @pl.when(first)
    def _():
        cnt_scr[...] = cnt0_ref[...]

    acc = None
    for j, p in enumerate((p0, p1, p2, p3)):
        part = jnp.dot(p[0], w_ref[j * GROUP_WIDTH:(j + 1) * GROUP_WIDTH, :],
                       preferred_element_type=F32)
        acc = part if acc is None else acc + part
    x = x_ref[0] + gt_ref[0] * acc
    xo_ref[0] = x
    ms = jnp.mean(x * x, axis=-1, keepdims=True)
    hf = (x * lax.rsqrt(ms + EPS) * g_ref[...]) * (1.0 + sc_ref[0]) + sh_ref[0]
    hf_ref[0] = hf

    tm = x.shape[0]
    hi = hf.astype(BF16)
    lo = (hf - hi.astype(F32)).astype(BF16)
    lg = jnp.dot(jnp.concatenate([hi, hi, lo], axis=1), wr3_ref[...], preferred_element_type=F32)
    lt = lg.T[:N_EXPERTS] + br_ref[...]
    row_e = lax.broadcasted_iota(jnp.int32, (N_EXPERTS, tm), 0).astype(F32)
    vals, idxs = [], []
    l = lt
    for _ in range(TOP_K):
        mx = jnp.max(l, axis=0, keepdims=True)
        ik = jnp.min(jnp.where(l == mx, row_e, float(N_EXPERTS)), axis=0, keepdims=True)
        vals.append(mx)
        idxs.append(ik)
        l = jnp.where(row_e == ik, -jnp.inf, l)
    exps = [jnp.exp(v - vals[0]) for v in vals]
    denom = exps[0] + exps[1] + exps[2] + exps[3]
    gates = [e / denom for e in exps]

    onehot = jnp.zeros((N_EXPERTS, tm), F32)
    for ik in idxs:
        onehot = onehot + (row_e == ik).astype(F32)
    ri = lax.broadcasted_iota(jnp.int32, (tm, tm), 0)
    ci = lax.broadcasted_iota(jnp.int32, (tm, tm), 1)
    earlier = (ri < ci).astype(BF16)
    before = jnp.dot(onehot.astype(BF16), earlier, preferred_element_type=F32) + cnt_scr[...]
    ranks = [jnp.sum(jnp.where(row_e == ik, before, 0.0), axis=0, keepdims=True) for ik in idxs]
    cnt_scr[...] = cnt_scr[...] + jnp.sum(onehot, axis=1, keepdims=True)
    cnt_ref[...] = cnt_scr[...]

    def pack_rows(rows, n):
        r = lax.broadcasted_iota(jnp.int32, (n, tm), 0)
        o = jnp.zeros((n, tm), F32)
        for k, rv in enumerate(rows):
            o = jnp.where(r == k, rv, o)
        return o

    idx_ref[0] = pack_rows(idxs, 8).astype(jnp.int32)
    rank_ref[0] = pack_rows(ranks, 8).astype(jnp.int32)
    gate_ref[0] = pack_rows(gates, 128).T


def _router_weights(w_router):
    whi = w_router.astype(BF16)
    wlo = (w_router - whi.astype(F32)).astype(BF16)
    w3 = jnp.concatenate([whi, wlo, whi], axis=0)
    return jnp.pad(w3, ((0, 0), (0, 128 - N_EXPERTS)))


def _outproj_call(cnt0, parts, w_out_bf16, x, gt, g, sc, sh, wr3, b_router, *, tm):
    B, L, D = x.shape
    per_batch = gt.shape[0] > 1
    mod_map = (lambda b, i: (b, 0, 0)) if per_batch else (lambda b, i: (0, 0, 0))
    part_spec = pl.BlockSpec((1, tm, GROUP_WIDTH), lambda b, i: (b, i, 0))
    row_spec = pl.BlockSpec((1, tm, D), lambda b, i: (b, i, 0))
    gate_spec = pl.BlockSpec((1, tm, 128), lambda b, i: (b, i, 0))
    slot_spec = pl.BlockSpec((1, 8, tm), lambda b, i: (b, 0, i))
    cnt_spec = pl.BlockSpec((N_EXPERTS, 1), lambda b, i: (0, 0))
    return pl.pallas_call(
        _outproj_kernel,
        grid=(B, L // tm),
        in_specs=[cnt_spec, part_spec, part_spec, part_spec, part_spec,
                  pl.BlockSpec((D, D), lambda b, i: (0, 0)),
                  row_spec,
                  pl.BlockSpec((1, 1, D), mod_map),
                  pl.BlockSpec((1, D), lambda b, i: (0, 0)),
                  pl.BlockSpec((1, 1, D), mod_map),
                  pl.BlockSpec((1, 1, D), mod_map),
                  pl.BlockSpec((3 * D, 128), lambda b, i: (0, 0)),
                  cnt_spec],
        out_specs=[row_spec, row_spec, slot_spec, gate_spec, slot_spec, cnt_spec],
        out_shape=[jax.ShapeDtypeStruct((B, L, D), F32),
                   jax.ShapeDtypeStruct((B, L, D), F32),
                   jax.ShapeDtypeStruct((B, 8, L), jnp.int32),
                   jax.ShapeDtypeStruct((B, L, 128), F32),
                   jax.ShapeDtypeStruct((B, 8, L), jnp.int32),
                   jax.ShapeDtypeStruct((N_EXPERTS, 1), F32)],
        scratch_shapes=[pltpu.VMEM((N_EXPERTS, 1), F32)],
        compiler_params=_cparams(("arbitrary", "arbitrary")),
        name="out_proj_router",
    )(cnt0, *parts, w_out_bf16, x, gt, g.reshape(1, D), sc, sh, wr3,
      b_router.reshape(N_EXPERTS, 1))


def _dispatch_kernel(ps_ref, cnt_ref, pad_ref, *refs, tm, tiles):
    n_streams = len(tiles)
    dest_refs = refs[:n_streams]
    hf_refs = refs[n_streams:2 * n_streams]
    xs_out, zbuf, sem, zsem = refs[2 * n_streams:]
    j = pl.program_id(0)
    n_blocks = xs_out.shape[0] // MOE_BM

    @pl.when(j == 0)
    def _():
        zbuf[...] = jnp.zeros(zbuf.shape, zbuf.dtype)

        def zero_row_copy(row):
            return pltpu.make_async_copy(zbuf.at[pl.ds(0, 1)], xs_out.at[pl.ds(row, 1)], zsem)

        def zero_block_copy(blk):
            row = pl.multiple_of(blk * MOE_BM, MOE_BM)
            return pltpu.make_async_copy(zbuf, xs_out.at[pl.ds(row, MOE_BM)], zsem)

        def per_expert(e, total):
            base = ps_ref[e] + cnt_ref[e]
            n = pad_ref[e] - cnt_ref[e]

            def per_row(r, carry):
                zero_row_copy(base + r).start()
                return carry

            lax.fori_loop(0, n, per_row, 0)
            return total + n

        n_rows = lax.fori_loop(0, N_EXPERTS, per_expert, 0)
        first_unused = (ps_ref[N_EXPERTS - 1] + pad_ref[N_EXPERTS - 1]) // MOE_BM

        def per_block(blk, carry):
            zero_block_copy(blk).start()
            return carry

        lax.fori_loop(first_unused, n_blocks, per_block, 0)

        def drain_row(r, carry):
            zero_row_copy(0).wait()
            return carry

        def drain_block(blk, carry):
            zero_block_copy(0).wait()
            return carry

        lax.fori_loop(0, n_rows, drain_row, 0)
        lax.fori_loop(first_unused, n_blocks, drain_block, 0)

    first_tile = 0
    for dest_ref, hf_ref, n_tiles in zip(dest_refs, hf_refs, tiles):
        @pl.when((j >= first_tile) & (j < first_tile + n_tiles))
        def _(dest_ref=dest_ref, hf_ref=hf_ref):
            def body(i, carry):
                for k in range(TOP_K):
                    pltpu.make_async_copy(hf_ref.at[0, pl.ds(i, 1)],
                                          xs_out.at[pl.ds(dest_ref[0, k, i], 1)],
                                          sem).start(priority=k % 2)
                return carry

            lax.fori_loop(0, tm, body, 0, unroll=4)
            for k in range(TOP_K):
                pltpu.make_async_copy(hf_ref.at[0], xs_out.at[pl.ds(0, tm)], sem).wait()
        first_tile += n_tiles


def _dispatch_call(pstart, counts, padded, dests, hfs, n_slots, *, tm):
    D = hfs[0].shape[-1]
    tiles, dest_specs, hf_specs = [], [], []
    first_tile = 0
    for hf in hfs:
        B, L, _ = hf.shape
        n_l = L // tm
        n_tiles = B * n_l

        def tile_of(j, first=first_tile, n=n_tiles):
            return jnp.clip(j - first, 0, n - 1)

        dest_specs.append(pl.BlockSpec(
            (1, 8, tm), lambda j, *_, t=tile_of, n_l=n_l: (t(j) // n_l, 0, t(j) % n_l),
            memory_space=pltpu.SMEM))
        hf_specs.append(pl.BlockSpec(
            (1, tm, D), lambda j, *_, t=tile_of, n_l=n_l: (t(j) // n_l, t(j) % n_l, 0)))
        tiles.append(n_tiles)
        first_tile += n_tiles
    return pl.pallas_call(
        functools.partial(_dispatch_kernel, tm=tm, tiles=tuple(tiles)),
        grid_spec=pltpu.PrefetchScalarGridSpec(
            num_scalar_prefetch=3, grid=(first_tile,),
            in_specs=dest_specs + hf_specs,
            out_specs=pl.BlockSpec(memory_space=pl.ANY),
            scratch_shapes=[pltpu.VMEM((MOE_BM, D), F32), pltpu.SemaphoreType.DMA(()),
                            pltpu.SemaphoreType.DMA(())]),
        out_shape=jax.ShapeDtypeStruct((n_slots, D), F32),
        compiler_params=_cparams(("arbitrary",)),
        name="moe_dispatch",
    )(pstart, counts, padded, *dests, *hfs)


def _expert_kernel(be_ref, na_ref, xs_ref, wgu_ref, bgu_ref, wd_ref, bd_ref, y_ref):
    del be_ref

    @pl.when(pl.program_id(0) < na_ref[0])
    def _():
        xb = xs_ref[...].astype(BF16)
        gu = jnp.dot(xb, wgu_ref[0], preferred_element_type=F32) + bgu_ref[0]
        gt = jnp.minimum(gu[:, :D_EXPERT], SWIGLU_LIMIT)
        up = jnp.clip(gu[:, D_EXPERT:], -SWIGLU_LIMIT, SWIGLU_LIMIT)
        act = (up + 1.0) * gt * (1.0 / (1.0 + jnp.exp(-SWIGLU_ALPHA * gt)))
        y_ref[...] = jnp.dot(act.astype(BF16), wd_ref[0], preferred_element_type=F32) + bd_ref[0]

    @pl.when(pl.program_id(0) >= na_ref[0])
    def _():
        y_ref[...] = jnp.zeros(y_ref.shape, y_ref.dtype)


def _expert_call(blk_e, n_active, xs, wgu, bgu, wd, bd):
    n_slots, D = xs.shape
    nb = n_slots // MOE_BM

    def row_map(i, be, na):
        return (jnp.minimum(i, jnp.maximum(na[0] - 1, 0)), 0)

    def w_map(i, be, na):
        return (be[jnp.minimum(i, jnp.maximum(na[0] - 1, 0))], 0, 0)

    return pl.pallas_call(
        _expert_kernel,
        grid_spec=pltpu.PrefetchScalarGridSpec(
            num_scalar_prefetch=2,
            grid=(nb,),
            in_specs=[pl.BlockSpec((MOE_BM, D), row_map),
                      pl.BlockSpec((1, D, 2 * D_EXPERT), w_map),
                      pl.BlockSpec((1, 1, 2 * D_EXPERT), w_map),
                      pl.BlockSpec((1, D_EXPERT, D), w_map),
                      pl.BlockSpec((1, 1, D), w_map)],
            out_specs=pl.BlockSpec((MOE_BM, D), lambda i, be, na: (i, 0))),
        out_shape=jax.ShapeDtypeStruct((n_slots, D), F32),
        compiler_params=_cparams(("arbitrary",)),
        name="moe_experts",
    )(blk_e, n_active, xs, wgu, bgu.reshape(N_EXPERTS, 1, 2 * D_EXPERT), wd,
      bd.reshape(N_EXPERTS, 1, D))


def _combine_kernel(dest_ref, gate_ref, x_ref, gt_ref, gfin_ref, yb_ref, o_ref, buf, sem, *,
                    tm, final_norm):
    def body(i, carry):
        for k in range(TOP_K):
            pltpu.make_async_copy(yb_ref.at[pl.ds(dest_ref[0, k, i], 1)], buf.at[k, pl.ds(i, 1)],
                                  sem).start(priority=k % 2)
        return carry

    lax.fori_loop(0, tm, body, 0, unroll=4)
    for k in range(TOP_K):
        pltpu.make_async_copy(yb_ref.at[pl.ds(0, tm)], buf.at[k], sem).wait()
    gates = gate_ref[0]
    y = buf[0] * gates[:, 0:1]
    for k in range(1, TOP_K):
        y = y + buf[k] * gates[:, k:k + 1]
    x = x_ref[0] + gt_ref[0] * y
    if final_norm:
        ms = jnp.mean(x * x, axis=-1, keepdims=True)
        x = x * lax.rsqrt(ms + EPS) * gfin_ref[...]
    o_ref[0] = x


def _combine_call(dest, gates, x, gt, g_final, yb, *, tm, final_norm):
    B, L, D = x.shape
    per_batch = gt.shape[0] > 1
    mod_map = (lambda b, i: (b, 0, 0)) if per_batch else (lambda b, i: (0, 0, 0))
    return pl.pallas_call(
        functools.partial(_combine_kernel, tm=tm, final_norm=final_norm),
        grid=(B, L // tm),
        in_specs=[pl.BlockSpec((1, 8, tm), lambda b, i: (b, 0, i), memory_space=pltpu.SMEM),
                  pl.BlockSpec((1, tm, 128), lambda b, i: (b, i, 0)),
                  pl.BlockSpec((1, tm, D), lambda b, i: (b, i, 0)),
                  pl.BlockSpec((1, 1, D), mod_map),
                  pl.BlockSpec((1, D), lambda b, i: (0, 0)),
                  pl.BlockSpec(memory_space=pl.ANY)],
        out_specs=pl.BlockSpec((1, tm, D), lambda b, i: (b, i, 0)),
        out_shape=jax.ShapeDtypeStruct((B, L, D), F32),
        scratch_shapes=[pltpu.VMEM((TOP_K, tm, D), F32), pltpu.SemaphoreType.DMA(())],
        compiler_params=_cparams(("arbitrary", "arbitrary")),
        name="moe_combine",
    )(dest, gates, x, gt, g_final.reshape(1, D), yb)


def _moe(streams, p_bf16, b_gu, b_down, g_final, final_norm):
    counts = streams[-1]["cnt"].reshape(N_EXPERTS).astype(jnp.int32)
    padded = (counts + MOE_BM - 1) // MOE_BM * MOE_BM
    cum_pad = jnp.cumsum(padded)
    pstart = cum_pad - padded
    total_tokens = sum(s["x"].shape[0] * s["x"].shape[1] for s in streams)
    nb = -(-total_tokens * TOP_K // MOE_BM) + N_EXPERTS
    blk_e = jnp.minimum(jnp.sum(cum_pad[None, :] <= jnp.arange(nb)[:, None] * MOE_BM, axis=1),
                        N_EXPERTS - 1).astype(jnp.int32)
    n_active = (cum_pad[-1] // MOE_BM).astype(jnp.int32).reshape(1)
    pstart = pstart.astype(jnp.int32)

    expert_ids = jnp.arange(N_EXPERTS, dtype=jnp.int32).reshape(N_EXPERTS, 1, 1, 1)
    dests = []
    for s in streams:
        seg = jnp.sum(jnp.where(s["idx"][None] == expert_ids, pstart.reshape(N_EXPERTS, 1, 1, 1), 0),
                      axis=0)
        dests.append(seg + s["rank"])
    xs = _dispatch_call(pstart, counts, padded.astype(jnp.int32), dests,
                        [s["hf"] for s in streams], nb * MOE_BM, tm=256)
    yb = _expert_call(blk_e, n_active, xs, p_bf16["w_gu"], b_gu, p_bf16["w_down"], b_down)
    outs = []
    for s, dest in zip(streams, dests):
        outs.append(_combine_call(dest, s["gate"], s["x"], s["gt"], g_final, yb,
                                  tm=min(128, s["x"].shape[1]), final_norm=final_norm))
    return outs


def _layer(x, xc, c8, p, layer_idx, update_ctx, tabs, g_final, final_norm):
    B, S, D = x.shape
    C = xc.shape[1]
    lam_init = 0.8 - 0.6 * math.exp(-0.3 * layer_idx)
    lam = (jnp.exp(jnp.sum(p["lam_q1"] * p["lam_k1"])) - jnp.exp(jnp.sum(p["lam_q2"] * p["lam_k2"]))
           + lam_init).reshape(1).astype(F32)

    mod8 = _mod_call(c8, p["w_mod"], p["b_mod"])
    mod = [m.reshape(B, 1, D) for m in jnp.split(mod8[:B], 6, axis=-1)]
    modc = [m.reshape(1, 1, D) for m in jnp.split(mod8[B:B + 1], 6, axis=-1)]
    sh1, sc1, gt1, sh2, sc2, gt2 = mod

    w_in = p["w_in"].astype(BF16)
    w_out = p["w_out"].astype(BF16)
    proj = _inproj_call(x, sc1, sh1, p["g_mix"], w_in, tabs["cos"], tabs["sin"], rope=True, tm=512)
    projc = _inproj_call(xc, modc[1], modc[0], p["g_mix"], w_in, tabs["cos"], tabs["sin"],
                         rope=False, tm=C)

    gsub_col = p["g_sub"].reshape(HEAD_DIM, 1)
    oa = _da_call(lam, proj, proj, projc, gsub_col, lam_init=lam_init, tq=256, tkc=512)
    bias = _na_bias_tables(p["rpb"], S // GRID_W)
    od = _na_call(proj, projc, bias)

    wpool_bd = jax.scipy.linalg.block_diag(*[p["w_pool"][g] for g in range(4)]).astype(BF16)
    spool = p["s_pool"].reshape(1, GROUP_WIDTH)
    gsgu = p["g_sgu"].reshape(1, GROUP_WIDTH)
    wsgu_cat = jnp.transpose(p["w_sgu"], (1, 0, 2)).reshape(SGU_CHUNK, N_HEADS * SGU_CHUNK).astype(BF16)
    bsgu_full = jnp.repeat(p["b_sgu"].T, HEAD_DIM, axis=1)
    pool, sgu = _mixers_call(proj, wpool_bd, spool, gsgu, wsgu_cat, bsgu_full, tm=512)

    cnt0 = jnp.zeros((N_EXPERTS, 1), F32)
    wr3 = _router_weights(p["w_router"])
    x1, hf, idx, gate, rank, cnt = _outproj_call(
        cnt0, (oa, pool, sgu, od), w_out, x, gt1, p["g_ffn"], sc2, sh2,
        wr3, p["b_router"], tm=256)
    streams = [dict(x=x1, hf=hf, idx=idx, gate=gate, rank=rank, gt=gt2, cnt=cnt)]

    if update_ctx:
        oa_c = _da_call(lam, projc, None, projc, gsub_col, lam_init=lam_init,
                        tq=C, tkc=512)
        od_c = _ctx_attn_call(projc)
        pool_c, sgu_c = _mixers_call(projc, wpool_bd, spool, gsgu, wsgu_cat, bsgu_full, tm=C)
        xc1, hfc, idxc, gatec, rankc, cnt = _outproj_call(
            cnt, (oa_c, pool_c, sgu_c, od_c), w_out, xc, modc[2], p["g_ffn"], modc[4], modc[3],
            wr3, p["b_router"], tm=C)
        streams.append(dict(x=xc1, hf=hfc, idx=idxc, gate=gatec, rank=rankc, gt=modc[5], cnt=cnt))

    p_bf16 = dict(w_gu=p["w_gu"].astype(BF16), w_down=p["w_down"].astype(BF16))
    outs = _moe(streams, p_bf16, p["b_gu"], p["b_down"], g_final, final_norm)
    if update_ctx:
        return outs[0], outs[1]
    return outs[0], xc


def kernel(x, c, ctx, c_ctx, w_mod, b_mod, g_mix, g_ffn, w_in, w_out, lam_q1, lam_k1, lam_q2,
           lam_k2, g_sub, w_pool, s_pool, g_sgu, w_sgu, b_sgu, rpb, w_router, b_router, w_gu,
           b_gu, w_down, b_down, g_final):
    B, S, D = x.shape
    cos_tab, sin_tab = _rope_tables(S)
    tabs = dict(cos=cos_tab, sin=sin_tab)
    c8 = jnp.concatenate([c, c_ctx[None, :], jnp.zeros((8 - B - 1, D), F32)], axis=0)
    xc = ctx
    for l in range(DEPTH):
        p = dict(w_mod=w_mod[l], b_mod=b_mod[l], g_mix=g_mix[l], g_ffn=g_ffn[l], w_in=w_in[l],
                 w_out=w_out[l], lam_q1=lam_q1[l], lam_k1=lam_k1[l], lam_q2=lam_q2[l],
                 lam_k2=lam_k2[l], g_sub=g_sub[l], w_pool=w_pool[l], s_pool=s_pool[l],
                 g_sgu=g_sgu[l], w_sgu=w_sgu[l], b_sgu=b_sgu[l], rpb=rpb[l],
                 w_router=w_router[l], b_router=b_router[l], w_gu=w_gu[l], b_gu=b_gu[l],
                 w_down=w_down[l], b_down=b_down[l])
        x, xc = _layer(x, xc, c8, p, l, l < DEPTH - 1, tabs, g_final, l == DEPTH - 1)
    return x
```

```python
import functools
import math

import jax
import jax.numpy as jnp
import numpy as np
from jax import lax
from jax.experimental import pallas as pl
from jax.experimental.pallas import tpu as pltpu

F32 = jnp.float32
BF16 = jnp.bfloat16

D_MODEL = 1024
DEPTH = 2
GRID_W = 64
HEAD_DIM = 64
N_HEADS = 4
GROUP_WIDTH = N_HEADS * HEAD_DIM
PROJ_WIDTH = 9 * GROUP_WIDTH
DA_QK = HEAD_DIM // 2
ROPE_BASE = 10000.0
POOL_WINDOWS = (2, 4, 8, 16)
SGU_CHUNK = 128
WIN_R = 8
WIN_C = 16
N_EXPERTS = 32
TOP_K = 4
D_EXPERT = D_MODEL
SWIGLU_LIMIT = 7.0
SWIGLU_ALPHA = 1.702
EPS = 1e-6
NEG_INF = -1e30
LOG2_E = math.log2(math.e)

COL_AQ, COL_AK, COL_AV, COL_POOL, COL_SGU, COL_NQ, COL_NK, COL_NV = 0, 1, 2, 3, 4, 6, 7, 8

NA_QROWS = 4
NA_BAND = 12
POOL_HALO = 16
MOE_BM = 256
VMEM_LIMIT = 48 * 1024 * 1024
EXPERT_VMEM_LIMIT = 56 * 1024 * 1024


def _cparams(sem):
    return pltpu.CompilerParams(dimension_semantics=sem, vmem_limit_bytes=VMEM_LIMIT)


def _lane_iota(n):
    return lax.broadcasted_iota(jnp.int32, (1, n), 1)


def _dot_nt(a, b):
    return lax.dot_general(a, b, (((1,), (1,)), ((), ())), preferred_element_type=F32)


def _mod_kernel(c_ref, w_ref, b_ref, o_ref):
    c = c_ref[...]
    s = c * (1.0 / (1.0 + jnp.exp(-c)))
    o_ref[...] = jnp.dot(s.astype(BF16), w_ref[...].astype(BF16),
                         preferred_element_type=F32) + b_ref[...]


def _mod_call(c8, w_mod, b_mod):
    n = w_mod.shape[1]
    tn = 1536
    return pl.pallas_call(
        _mod_kernel,
        grid=(n // tn,),
        in_specs=[pl.BlockSpec((8, D_MODEL), lambda j: (0, 0)),
                  pl.BlockSpec((D_MODEL, tn), lambda j: (0, j)),
                  pl.BlockSpec((1, tn), lambda j: (0, j))],
        out_specs=pl.BlockSpec((8, tn), lambda j: (0, j)),
        out_shape=jax.ShapeDtypeStruct((8, n), F32),
        compiler_params=_cparams(("arbitrary",)),
        name="adaln_mod",
    )(c8, w_mod, b_mod.reshape(1, n))


def _swap8(x):
    n = x.shape[-1]
    lane = _lane_iota(n)
    first_half = (lane & 15) < 8
    return jnp.where(first_half, pltpu.roll(x, n - 8, axis=1), pltpu.roll(x, 8, axis=1))


def _inproj_kernel(x_ref, sc_ref, sh_ref, g_ref, w_ref, cos_ref, sin_ref, o_ref, *, rope):
    x = x_ref[0]
    ms = jnp.mean(x * x, axis=-1, keepdims=True)
    y = x * lax.rsqrt(ms + EPS) * g_ref[...]
    h = (y * (1.0 + sc_ref[0]) + sh_ref[0]).astype(BF16)
    for j in range(PROJ_WIDTH // GROUP_WIDTH):
        cols = slice(j * GROUP_WIDTH, (j + 1) * GROUP_WIDTH)
        acc = jnp.dot(h, w_ref[:, cols], preferred_element_type=F32)
        if rope and j in (COL_AQ, COL_AK):
            acc = acc * cos_ref[...] + _swap8(acc) * sin_ref[...]
        if j == COL_AQ:
            acc = acc * (DA_QK ** -0.5 * LOG2_E)
        if j == COL_NQ:
            acc = acc * (HEAD_DIM ** -0.5)
        o_ref[0, :, cols] = acc.astype(o_ref.dtype)


def _inproj_call(x, sc, sh, g, w_bf16, cos_tab, sin_tab, *, rope, tm):
    B, L, D = x.shape
    per_batch = sc.shape[0] > 1
    mod_map = (lambda b, i: (b, 0, 0)) if per_batch else (lambda b, i: (0, 0, 0))
    tab_map = (lambda b, i: (i, 0)) if rope else (lambda b, i: (0, 0))
    return pl.pallas_call(
        functools.partial(_inproj_kernel, rope=rope),
        grid=(B, L // tm),
        in_specs=[pl.BlockSpec((1, tm, D), lambda b, i: (b, i, 0)),
                  pl.BlockSpec((1, 1, D), mod_map),
                  pl.BlockSpec((1, 1, D), mod_map),
                  pl.BlockSpec((1, D), lambda b, i: (0, 0)),
                  pl.BlockSpec((D, PROJ_WIDTH), lambda b, i: (0, 0)),
                  pl.BlockSpec((tm, GROUP_WIDTH), tab_map),
                  pl.BlockSpec((tm, GROUP_WIDTH), tab_map)],
        out_specs=pl.BlockSpec((1, tm, PROJ_WIDTH), lambda b, i: (b, i, 0)),
        out_shape=jax.ShapeDtypeStruct((B, L, PROJ_WIDTH), BF16),
        compiler_params=_cparams(("arbitrary", "arbitrary")),
        name="in_proj",
    )(x, sc, sh, g.reshape(1, D), w_bf16, cos_tab, sin_tab)


def _rope_tables(S):
    n = 8
    inv = ROPE_BASE ** (-jnp.arange(n, dtype=F32) / n)
    t = jnp.arange(S)
    row_pos = (t // GRID_W).astype(F32)
    col_pos = (t % GRID_W).astype(F32)
    parts_c, parts_s = [], []
    for pos in (row_pos, col_pos):
        ang = pos[:, None] * inv[None, :]
        c, s = jnp.cos(ang), jnp.sin(ang)
        parts_c += [c, c]
        parts_s += [-s, s]
    cos32 = jnp.concatenate(parts_c, axis=1)
    sin32 = jnp.concatenate(parts_s, axis=1)
    reps = GROUP_WIDTH // 32
    return jnp.tile(cos32, (1, reps)), jnp.tile(sin32, (1, reps))


def _da_kernel(lam_ref, *refs, n_lat, tkc, post_scale):
    if n_lat:
        q_ref, kl_ref, vlt_ref, kc_ref, vct_ref, gsub_ref, o_ref = refs[:7]
    else:
        q_ref, kc_ref, vct_ref, gsub_ref, o_ref = refs[:5]
    qs_scr, m_scr, l_scr, acc_scr = refs[-4:]
    q = q_ref[0]
    tq = q.shape[0]
    lane = _lane_iota(GROUP_WIDTH)
    for g in range(2 * N_HEADS):
        sel = (lane >= g * DA_QK) & (lane < (g + 1) * DA_QK)
        qs_scr[g * tq:(g + 1) * tq, :] = q * sel.astype(BF16)
    m_scr[...] = jnp.full(m_scr.shape, NEG_INF, F32)
    l_scr[...] = jnp.zeros(l_scr.shape, F32)
    acc_scr[...] = jnp.zeros(acc_scr.shape, F32)

    def step(k, vt):
        def scores(h):
            return _dot_nt(k, qs_scr[2 * h * tq:(2 * h + 2) * tq, :])

        st_next = scores(0)
        for h in range(N_HEADS):
            rows = slice(h * HEAD_DIM, (h + 1) * HEAD_DIM)
            cols = slice(2 * h * tq, (2 * h + 2) * tq)
            st = st_next
            if h + 1 < N_HEADS:
                st_next = scores(h + 1)
            m_old = m_scr[:, cols]
            m_new = jnp.maximum(m_old, jnp.max(st, axis=0, keepdims=True))
            alpha = jnp.exp2(m_old - m_new)
            p = jnp.exp2(st - m_new)
            l_scr[:, cols] = alpha * l_scr[:, cols] + jnp.sum(p, axis=0, keepdims=True)
            m_scr[:, cols] = m_new
            pv = jnp.dot(vt[rows, :], p.astype(BF16), preferred_element_type=F32)
            acc_scr[rows, :] = alpha * acc_scr[rows, :] + pv

    if n_lat:
        def body(c, carry):
            off = pl.multiple_of(c * tkc, tkc)
            step(kl_ref[0, pl.ds(off, tkc), :], vlt_ref[0, c])
            return carry
        lax.fori_loop(0, n_lat, body, 0)
    step(kc_ref[0], vct_ref[0, 0])

    lam = lam_ref[0]
    outs = []
    for h in range(N_HEADS):
        rows = slice(h * HEAD_DIM, (h + 1) * HEAD_DIM)
        cols = slice(2 * h * tq, (2 * h + 2) * tq)
        o = acc_scr[rows, :] / l_scr[:, cols]
        o_h = o[:, :tq] - lam * o[:, tq:]
        ms = jnp.mean(o_h * o_h, axis=0, keepdims=True)
        outs.append(o_h * lax.rsqrt(ms + EPS) * gsub_ref[...] * post_scale)
    o_ref[0] = jnp.concatenate(outs, axis=0).T.astype(o_ref.dtype)


def _values_t(proj, tkc):
    B, L, _ = proj.shape
    v = proj[:, :, COL_AV * GROUP_WIDTH:(COL_AV + 1) * GROUP_WIDTH]
    return jnp.transpose(v.reshape(B, L // tkc, tkc, GROUP_WIDTH), (0, 1, 3, 2))


def _da_call(lam, proj_q, proj_lat, proj_ctx, gsub_col, *, lam_init, tq, tkc):
    B, Lq, _ = proj_q.shape
    C = proj_ctx.shape[1]
    in_specs = [pl.BlockSpec(memory_space=pltpu.SMEM),
                pl.BlockSpec((1, tq, GROUP_WIDTH), lambda b, i: (b, i, COL_AQ))]
    args = [lam, proj_q]
    n_lat = 0
    if proj_lat is not None:
        S = proj_lat.shape[1]
        n_lat = S // tkc
        in_specs += [pl.BlockSpec((1, S, GROUP_WIDTH), lambda b, i: (b, 0, COL_AK)),
                     pl.BlockSpec((1, n_lat, GROUP_WIDTH, tkc), lambda b, i: (b, 0, 0, 0))]
        args += [proj_lat, _values_t(proj_lat, tkc)]
    in_specs += [pl.BlockSpec((1, C, GROUP_WIDTH), lambda b, i: (b, 0, COL_AK)),
                 pl.BlockSpec((1, 1, GROUP_WIDTH, C), lambda b, i: (b, 0, 0, 0)),
                 pl.BlockSpec((HEAD_DIM, 1), lambda b, i: (0, 0))]
    args += [proj_ctx, _values_t(proj_ctx, C), gsub_col]
    n_stack = 2 * N_HEADS * tq
    return pl.pallas_call(
        functools.partial(_da_kernel, n_lat=n_lat, tkc=tkc, post_scale=1.0 - lam_init),
        grid=(B, Lq // tq),
        in_specs=in_specs,
        out_specs=pl.BlockSpec((1, tq, GROUP_WIDTH), lambda b, i: (b, i, 0)),
        out_shape=jax.ShapeDtypeStruct((B, Lq, GROUP_WIDTH), BF16),
        scratch_shapes=[pltpu.VMEM((n_stack, GROUP_WIDTH), BF16),
                        pltpu.VMEM((1, n_stack), F32),
                        pltpu.VMEM((1, n_stack), F32),
                        pltpu.VMEM((GROUP_WIDTH, 2 * tq), F32)],
        compiler_params=_cparams(("arbitrary", "arbitrary")),
        name="diff_attn" if n_lat else "diff_attn_ctx",
    )(*args)


def _softmax_heads(q, sources, out_dtype):
    tq = q.shape[0]
    lane = _lane_iota(GROUP_WIDTH)
    out = jnp.zeros((tq, GROUP_WIDTH), F32)
    for h in range(N_HEADS):
        hm = (lane >= h * HEAD_DIM) & (lane < (h + 1) * HEAD_DIM)
        qm = q * hm.astype(BF16)
        scores = []
        for k, _, bias_fn in sources:
            s = _dot_nt(qm, k)
            if bias_fn is not None:
                s = s + bias_fn(h)
            scores.append(s)
        m = scores[0].max(axis=-1, keepdims=True)
        for s in scores[1:]:
            m = jnp.maximum(m, s.max(axis=-1, keepdims=True))
        l = jnp.zeros((tq, 1), F32)
        o = jnp.zeros((tq, GROUP_WIDTH), F32)
        for s, (_, v, _) in zip(scores, sources):
            p = jnp.exp(s - m)
            l = l + jnp.sum(p, axis=-1, keepdims=True)
            o = o + jnp.dot(p.astype(BF16), v, preferred_element_type=F32)
        out = jnp.where(hm, o / l, out)
    return out.astype(out_dtype)


def _na_kernel(q_ref, kn_ref, vn_ref, kc_ref, vc_ref, bias_ref, o_ref, *, n_groups):
    g = pl.program_id(1)
    tq = NA_QROWS * GRID_W
    band = NA_BAND * GRID_W
    start = pl.multiple_of(jnp.clip(g - 1, 0, n_groups - NA_BAND // NA_QROWS) * tq, tq)
    kb = kn_ref[0, pl.ds(start, band), :]
    vb = vn_ref[0, pl.ds(start, band), :]
    sources = [(kb, vb, lambda h: bias_ref[0, h]), (kc_ref[0], vc_ref[0], None)]
    o_ref[0] = _softmax_heads(q_ref[0], sources, o_ref.dtype)


def _na_call(proj, proj_ctx, bias):
    B, S, _ = proj.shape
    C = proj_ctx.shape[1]
    tq = NA_QROWS * GRID_W
    band = NA_BAND * GRID_W
    n_groups = S // tq

    def bias_map(b, g):
        return (jnp.where(g == 0, 0, jnp.where(g == n_groups - 1, 2, 1)), 0, 0, 0)

    return pl.pallas_call(
        functools.partial(_na_kernel, n_groups=n_groups),
        grid=(B, n_groups),
        in_specs=[pl.BlockSpec((1, tq, GROUP_WIDTH), lambda b, g: (b, g, COL_NQ)),
                  pl.BlockSpec((1, S, GROUP_WIDTH), lambda b, g: (b, 0, COL_NK)),
                  pl.BlockSpec((1, S, GROUP_WIDTH), lambda b, g: (b, 0, COL_NV)),
                  pl.BlockSpec((1, C, GROUP_WIDTH), lambda b, g: (b, 0, COL_NK)),
                  pl.BlockSpec((1, C, GROUP_WIDTH), lambda b, g: (b, 0, COL_NV)),
                  pl.BlockSpec((1, N_HEADS, tq, band), bias_map)],
        out_specs=pl.BlockSpec((1, tq, GROUP_WIDTH), lambda b, g: (b, g, 0)),
        out_shape=jax.ShapeDtypeStruct((B, S, GROUP_WIDTH), BF16),
        compiler_params=_cparams(("arbitrary", "arbitrary")),
        name="nbr_attn",
    )(proj, proj, proj, proj_ctx, proj_ctx, bias)


def _na_bias_tables(rpb, rows):
    n_groups = rows // NA_QROWS
    n_rr, n_rc = 2 * WIN_R - 1, 2 * WIN_C - 1
    c = np.arange(GRID_W)[:, None]
    kc = np.arange(GRID_W)[None, :]
    col_lo = np.clip(c - WIN_C // 2, 0, GRID_W - WIN_C)
    col_ok = (kc >= col_lo) & (kc < col_lo + WIN_C)
    rel_c = np.clip(kc - c, -(WIN_C - 1), WIN_C - 1) + (WIN_C - 1)
    sel_c = (rel_c[..., None] == np.arange(n_rc)).astype(np.float32)
    sel_r, ok = [], []
    for g in (0, 1, n_groups - 1):
        r0 = g * NA_QROWS
        start = min(max(r0 - WIN_R // 2, 0), rows - NA_BAND)
        r = r0 + np.arange(NA_QROWS)[:, None]
        kr = start + np.arange(NA_BAND)[None, :]
        row_lo = np.clip(r - WIN_R // 2, 0, rows - WIN_R)
        row_ok = (kr >= row_lo) & (kr < row_lo + WIN_R)
        rel_r = kr - r + (WIN_R - 1)
        sel_r.append((rel_r[..., None] == np.arange(n_rr)).astype(np.float32))
        ok.append(row_ok[:, None, :, None] & col_ok[None, :, None, :])
    sel_r = jnp.asarray(np.stack(sel_r))
    ok = np.stack(ok).reshape(3, 1, NA_QROWS * GRID_W, NA_BAND * GRID_W)
    vals = jnp.einsum("vxka,hab,cyb->vhxcky", sel_r, rpb.astype(F32), jnp.asarray(sel_c),
                      precision=lax.Precision.HIGHEST)
    vals = vals.reshape(3, N_HEADS, NA_QROWS * GRID_W, NA_BAND * GRID_W)
    return jnp.where(jnp.asarray(ok), vals, NEG_INF)


def _ctx_attn_kernel(q_ref, k_ref, v_ref, o_ref):
    o_ref[0] = _softmax_heads(q_ref[0], [(k_ref[0], v_ref[0], None)], o_ref.dtype)


def _ctx_attn_call(proj_ctx):
    B, C, _ = proj_ctx.shape
    return pl.pallas_call(
        _ctx_attn_kernel,
        grid=(B,),
        in_specs=[pl.BlockSpec((1, C, GROUP_WIDTH), lambda b: (b, 0, COL_NQ)),
                  pl.BlockSpec((1, C, GROUP_WIDTH), lambda b: (b, 0, COL_NK)),
                  pl.BlockSpec((1, C, GROUP_WIDTH), lambda b: (b, 0, COL_NV))],
        out_specs=pl.BlockSpec((1, C, GROUP_WIDTH), lambda b: (b, 0, 0)),
        out_shape=jax.ShapeDtypeStruct((B, C, GROUP_WIDTH), BF16),
        compiler_params=_cparams(("arbitrary",)),
        name="ctx_attn",
    )(proj_ctx, proj_ctx, proj_ctx)


def _gelu_tanh(x):
    return 0.5 * x * (1.0 + jnp.tanh(math.sqrt(2.0 / math.pi) * (x + 0.044715 * (x * x * x))))


def _mixers_kernel(pc_ref, pp_ref, pn_ref, sg_ref, wpool_ref, spool_ref, gsgu_ref, wsgu_ref,
                   bsgu_ref, pool_o, sgu_o, *, tm, seq_len):
    i = pl.program_id(1)
    n_tiles = seq_len // tm
    lane = _lane_iota(GROUP_WIDTH)
    grp = lax.shift_right_logical(lane, 6)

    def by_group(vals):
        return jnp.where(grp == 0, vals[0], jnp.where(grp == 1, vals[1],
                                                       jnp.where(grp == 2, vals[2], vals[3])))

    cur = pc_ref[0].astype(F32)
    prev = jnp.where(i > 0, pp_ref[0].astype(F32), 0.0)
    nxt = jnp.where(i < n_tiles - 1, pn_ref[0].astype(F32), 0.0)
    ext = jnp.concatenate([prev, cur, nxt], axis=0)
    n = tm + 2 * POOL_HALO

    def ahead(x, d):
        return pltpu.roll(x, n - d, axis=0)

    def behind(x, d):
        return pltpu.roll(x, d, axis=0)

    f2 = ext + ahead(ext, 1)
    f4 = f2 + ahead(f2, 2)
    f8 = f4 + ahead(f4, 4)
    f16 = f8 + ahead(f8, 8)
    sums = by_group([behind(f2, 1), behind(f4, 2), behind(f8, 4), behind(f16, 8)])
    sums = sums[POOL_HALO:POOL_HALO + tm]
    t = i * tm + lax.broadcasted_iota(jnp.int32, (tm, 1), 0)
    cnts = []
    for w in POOL_WINDOWS:
        lo = jnp.maximum(t - w // 2, 0)
        hi = jnp.minimum(t + (w - w // 2 - 1), seq_len - 1)
        cnts.append((hi - lo + 1).astype(F32))
    cnt = by_group(cnts)
    diff = (sums / cnt - cur).astype(BF16)
    pool = jnp.dot(diff, wpool_ref[...], preferred_element_type=F32) * spool_ref[...]
    pool_o[0] = pool.astype(pool_o.dtype)

    z = _gelu_tanh(sg_ref[0].astype(F32))
    u = z[:, :GROUP_WIDTH]
    v = z[:, GROUP_WIDTH:]
    v = v * lax.rsqrt(jnp.mean(v * v, axis=-1, keepdims=True) + EPS) * gsgu_ref[...]
    v = v.astype(BF16)
    hmasks = [(grp == h).astype(BF16) for h in range(N_HEADS)]
    for c in range(tm // SGU_CHUNK):
        rows = slice(c * SGU_CHUNK, (c + 1) * SGU_CHUNK)
        vc = v[rows]
        vstack = jnp.concatenate([vc * hm for hm in hmasks], axis=0)
        s = jnp.dot(wsgu_ref[...], vstack, preferred_element_type=F32) + bsgu_ref[...]
        sgu_o[0, rows, :] = (u[rows] * s).astype(sgu_o.dtype)


def _mixers_call(proj, wpool_bd, spool, gsgu, wsgu_cat, bsgu_full, *, tm):
    B, L, _ = proj.shape
    hb = tm // POOL_HALO
    n_halo = L // POOL_HALO
    return pl.pallas_call(
        functools.partial(_mixers_kernel, tm=tm, seq_len=L),
        grid=(B, L // tm),
        in_specs=[pl.BlockSpec((1, tm, GROUP_WIDTH), lambda b, i: (b, i, COL_POOL)),
                  pl.BlockSpec((1, POOL_HALO, GROUP_WIDTH),
                               lambda b, i: (b, jnp.maximum(i * hb - 1, 0), COL_POOL)),
                  pl.BlockSpec((1, POOL_HALO, GROUP_WIDTH),
                               lambda b, i: (b, jnp.minimum((i + 1) * hb, n_halo - 1), COL_POOL)),
                  pl.BlockSpec((1, tm, 2 * GROUP_WIDTH), lambda b, i: (b, i, COL_SGU // 2)),
                  pl.BlockSpec((GROUP_WIDTH, GROUP_WIDTH), lambda b, i: (0, 0)),
                  pl.BlockSpec((1, GROUP_WIDTH), lambda b, i: (0, 0)),
                  pl.BlockSpec((1, GROUP_WIDTH), lambda b, i: (0, 0)),
                  pl.BlockSpec((SGU_CHUNK, N_HEADS * SGU_CHUNK), lambda b, i: (0, 0)),
                  pl.BlockSpec((SGU_CHUNK, GROUP_WIDTH), lambda b, i: (0, 0))],
        out_specs=[pl.BlockSpec((1, tm, GROUP_WIDTH), lambda b, i: (b, i, 0)),
                   pl.BlockSpec((1, tm, GROUP_WIDTH), lambda b, i: (b, i, 0))],
        out_shape=[jax.ShapeDtypeStruct((B, L, GROUP_WIDTH), BF16),
                   jax.ShapeDtypeStruct((B, L, GROUP_WIDTH), BF16)],
        compiler_params=_cparams(("arbitrary", "arbitrary")),
        name="pool_sgu",
    )(proj, proj, proj, proj, wpool_bd, spool, gsgu, wsgu_cat, bsgu_full)


def _outproj_kernel(cnt0_ref, p0, p1, p2, p3, w_ref, x_ref, gt_ref, g_ref, sc_ref, sh_ref,
                    wr3_ref, br_ref, xo_ref, hf_ref, idx_ref, gate_ref, rank_ref, cnt_ref, cnt_scr):
    first = (pl.program_id(0) == 0) & (pl.program_id(1) == 0)

    @pl.when(first)
    def _():
        cnt_scr[...] = cnt0_ref[...]

    acc = None
    for j, p in enumerate((p0, p1, p2, p3)):
        part = jnp.dot(p[0], w_ref[j * GROUP_WIDTH:(j + 1) * GROUP_WIDTH, :],
                       preferred_element_type=F32)
        acc = part if acc is None else acc + part
    x = x_ref[0] + gt_ref[0] * acc
    xo_ref[0] = x
    ms = jnp.mean(x * x, axis=-1, keepdims=True)
    hf = (x * lax.rsqrt(ms + EPS) * g_ref[...]) * (1.0 + sc_ref[0]) + sh_ref[0]
    hf_ref[0] = hf

    tm = x.shape[0]
    hi = hf.astype(BF16)
    lo = (hf - hi.astype(F32)).astype(BF16)
    lg = jnp.dot(jnp.concatenate([hi, hi, lo], axis=1), wr3_ref[...], preferred_element_type=F32)
    lt = lg.T[:N_EXPERTS] + br_ref[...]
    row_e = lax.broadcasted_iota(jnp.int32, (N_EXPERTS, tm), 0).astype(F32)
    vals, idxs = [], []
    l = lt
    for _ in range(TOP_K):
        mx = jnp.max(l, axis=0, keepdims=True)
        ik = jnp.min(jnp.where(l == mx, row_e, float(N_EXPERTS)), axis=0, keepdims=True)
        vals.append(mx)
        idxs.append(ik)
        l = jnp.where(row_e == ik, -jnp.inf, l)
    exps = [jnp.exp(v - vals[0]) for v in vals]
    denom = exps[0] + exps[1] + exps[2] + exps[3]
    gates = [e / denom for e in exps]

    onehot = jnp.zeros((N_EXPERTS, tm), F32)
    for ik in idxs:
        onehot = onehot + (row_e == ik).astype(F32)
    ri = lax.broadcasted_iota(jnp.int32, (tm, tm), 0)
    ci = lax.broadcasted_iota(jnp.int32, (tm, tm), 1)
    earlier = (ri < ci).astype(BF16)
    before = jnp.dot(onehot.astype(BF16), earlier, preferred_element_type=F32) + cnt_scr[...]
    ranks = [jnp.sum(jnp.where(row_e == ik, before, 0.0), axis=0, keepdims=True) for ik in idxs]
    cnt_scr[...] = cnt_scr[...] + jnp.sum(onehot, axis=1, keepdims=True)
    cnt_ref[...] = cnt_scr[...]

    def pack_rows(rows, n):
        r = lax.broadcasted_iota(jnp.int32, (n, tm), 0)
        o = jnp.zeros((n, tm), F32)
        for k, rv in enumerate(rows):
            o = jnp.where(r == k, rv, o)
        return o

    idx_ref[0] = pack_rows(idxs, 8).astype(jnp.int32)
    rank_ref[0] = pack_rows(ranks, 8).astype(jnp.int32)
    gate_ref[0] = pack_rows(gates, 128).T


def _router_weights(w_router):
    whi = w_router.astype(BF16)
    wlo = (w_router - whi.astype(F32)).astype(BF16)
    w3 = jnp.concatenate([whi, wlo, whi], axis=0)
    return jnp.pad(w3, ((0, 0), (0, 128 - N_EXPERTS)))


def _outproj_call(cnt0, parts, w_out_bf16, x, gt, g, sc, sh, wr3, b_router, *, tm):
    B, L, D = x.shape
    per_batch = gt.shape[0] > 1
    mod_map = (lambda b, i: (b, 0, 0)) if per_batch else (lambda b, i: (0, 0, 0))
    part_spec = pl.BlockSpec((1, tm, GROUP_WIDTH), lambda b, i: (b, i, 0))
    row_spec = pl.BlockSpec((1, tm, D), lambda b, i: (b, i, 0))
    gate_spec = pl.BlockSpec((1, tm, 128), lambda b, i: (b, i, 0))
    slot_spec = pl.BlockSpec((1, 8, tm), lambda b, i: (b, 0, i))
    cnt_spec = pl.BlockSpec((N_EXPERTS, 1), lambda b, i: (0, 0))
    return pl.pallas_call(
        _outproj_kernel,
        grid=(B, L // tm),
        in_specs=[cnt_spec, part_spec, part_spec, part_spec, part_spec,
                  pl.BlockSpec((D, D), lambda b, i: (0, 0)),
                  row_spec,
                  pl.BlockSpec((1, 1, D), mod_map),
                  pl.BlockSpec((1, D), lambda b, i: (0, 0)),
                  pl.BlockSpec((1, 1, D), mod_map),
                  pl.BlockSpec((1, 1, D), mod_map),
                  pl.BlockSpec((3 * D, 128), lambda b, i: (0, 0)),
                  cnt_spec],
        out_specs=[row_spec, row_spec, slot_spec, gate_spec, slot_spec, cnt_spec],
        out_shape=[jax.ShapeDtypeStruct((B, L, D), F32),
                   jax.ShapeDtypeStruct((B, L, D), F32),
                   jax.ShapeDtypeStruct((B, 8, L), jnp.int32),
                   jax.ShapeDtypeStruct((B, L, 128), F32),
                   jax.ShapeDtypeStruct((B, 8, L), jnp.int32),
                   jax.ShapeDtypeStruct((N_EXPERTS, 1), F32)],
        scratch_shapes=[pltpu.VMEM((N_EXPERTS, 1), F32)],
        compiler_params=_cparams(("arbitrary", "arbitrary")),
        name="out_proj_router",
    )(cnt0, *parts, w_out_bf16, x, gt, g.reshape(1, D), sc, sh, wr3,
      b_router.reshape(N_EXPERTS, 1))


def _dispatch_kernel(ps_ref, cnt_ref, pad_ref, *refs, tm, tiles):
    n_streams = len(tiles)
    dest_refs = refs[:n_streams]
    hf_refs = refs[n_streams:2 * n_streams]
    xs_out, zbuf, sem, zsem = refs[2 * n_streams:]
    j = pl.program_id(0)
    n_blocks = xs_out.shape[0] // MOE_BM

    @pl.when(j == 0)
    def _():
        zbuf[...] = jnp.zeros(zbuf.shape, zbuf.dtype)

        def zero_row_copy(row):
            return pltpu.make_async_copy(zbuf.at[pl.ds(0, 1)], xs_out.at[pl.ds(row, 1)], zsem)

        def zero_block_copy(blk):
            row = pl.multiple_of(blk * MOE_BM, MOE_BM)
            return pltpu.make_async_copy(zbuf, xs_out.at[pl.ds(row, MOE_BM)], zsem)

        def per_expert(e, total):
            base = ps_ref[e] + cnt_ref[e]
            n = pad_ref[e] - cnt_ref[e]

            def per_row(r, carry):
                zero_row_copy(base + r).start()
                return carry

            lax.fori_loop(0, n, per_row, 0)
            return total + n

        n_rows = lax.fori_loop(0, N_EXPERTS, per_expert, 0)
        first_unused = (ps_ref[N_EXPERTS - 1] + pad_ref[N_EXPERTS - 1]) // MOE_BM

        def per_block(blk, carry):
            zero_block_copy(blk).start()
            return carry

        lax.fori_loop(first_unused, n_blocks, per_block, 0)

        def drain_row(r, carry):
            zero_row_copy(0).wait()
            return carry

        def drain_block(blk, carry):
            zero_block_copy(0).wait()
            return carry

        lax.fori_loop(0, n_rows, drain_row, 0)
        lax.fori_loop(first_unused, n_blocks, drain_block, 0)

    first_tile = 0
    for dest_ref, hf_ref, n_tiles in zip(dest_refs, hf_refs, tiles):
        @pl.when((j >= first_tile) & (j < first_tile + n_tiles))
        def _(dest_ref=dest_ref, hf_ref=hf_ref):
            def body(i, carry):
                for k in range(TOP_K):
                    pltpu.make_async_copy(hf_ref.at[0, pl.ds(i, 1)],
                                          xs_out.at[pl.ds(dest_ref[0, k, i], 1)],
                                          sem).start(priority=k % 2)
                return carry

            lax.fori_loop(0, tm, body, 0, unroll=4)
            for k in range(TOP_K):
                pltpu.make_async_copy(hf_ref.at[0], xs_out.at[pl.ds(0, tm)], sem).wait()
        first_tile += n_tiles


def _dispatch_call(pstart, counts, padded, dests, hfs, n_slots, *, tm):
    D = hfs[0].shape[-1]
    tiles, dest_specs, hf_specs = [], [], []
    first_tile = 0
    for hf in hfs:
        B, L, _ = hf.shape
        n_l = L // tm
        n_tiles = B * n_l

        def tile_of(j, first=first_tile, n=n_tiles):
            return jnp.clip(j - first, 0, n - 1)

        dest_specs.append(pl.BlockSpec(
            (1, 8, tm), lambda j, *_, t=tile_of, n_l=n_l: (t(j) // n_l, 0, t(j) % n_l),
            memory_space=pltpu.SMEM))
        hf_specs.append(pl.BlockSpec(
            (1, tm, D), lambda j, *_, t=tile_of, n_l=n_l: (t(j) // n_l, t(j) % n_l, 0)))
        tiles.append(n_tiles)
        first_tile += n_tiles
    return pl.pallas_call(
        functools.partial(_dispatch_kernel, tm=tm, tiles=tuple(tiles)),
        grid_spec=pltpu.PrefetchScalarGridSpec(
            num_scalar_prefetch=3, grid=(first_tile,),
            in_specs=dest_specs + hf_specs,
            out_specs=pl.BlockSpec(memory_space=pl.ANY),
            scratch_shapes=[pltpu.VMEM((MOE_BM, D), F32), pltpu.SemaphoreType.DMA(()),
                            pltpu.SemaphoreType.DMA(())]),
        out_shape=jax.ShapeDtypeStruct((n_slots, D), F32),
        compiler_params=_cparams(("arbitrary",)),
        name="moe_dispatch",
    )(pstart, counts, padded, *dests, *hfs)


def _expert_kernel(be_ref, na_ref, xs_ref, wgu_ref, bgu_ref, wd_ref, bd_ref, y_ref,
                   wgu_bf, wd_bf):
    i = pl.program_id(0)
    active = i < na_ref[0]
    new_expert = (i == 0) | (be_ref[i] != be_ref[jnp.maximum(i - 1, 0)])

    @pl.when(active & new_expert)
    def _():
        wgu_bf[...] = wgu_ref[0, 0].astype(BF16)
        wd_bf[...] = wd_ref[0, 0].astype(BF16)

    @pl.when(active)
    def _():
        xb = xs_ref[...].astype(BF16)
        gu = jnp.dot(xb, wgu_bf[...], preferred_element_type=F32) + bgu_ref[0, 0]
        gt = jnp.minimum(gu[:, :D_EXPERT], SWIGLU_LIMIT)
        up = jnp.clip(gu[:, D_EXPERT:], -SWIGLU_LIMIT, SWIGLU_LIMIT)
        act = (up + 1.0) * gt * (1.0 / (1.0 + jnp.exp(-SWIGLU_ALPHA * gt)))
        y_ref[...] = jnp.dot(act.astype(BF16), wd_bf[...], preferred_element_type=F32) + bd_ref[0, 0]

    @pl.when(jnp.logical_not(active))
    def _():
        y_ref[...] = jnp.zeros(y_ref.shape, y_ref.dtype)


def _expert_call(layer, blk_e, n_active, xs, w_gu, b_gu, w_down, b_down):
    n_slots, D = xs.shape
    nb = n_slots // MOE_BM

    def row_map(i, be, na):
        return (jnp.minimum(i, jnp.maximum(na[0] - 1, 0)), 0)

    def w_map(i, be, na):
        return (layer, be[jnp.minimum(i, jnp.maximum(na[0] - 1, 0))], 0, 0)

    return pl.pallas_call(
        _expert_kernel,
        grid_spec=pltpu.PrefetchScalarGridSpec(
            num_scalar_prefetch=2,
            grid=(nb,),
            in_specs=[pl.BlockSpec((MOE_BM, D), row_map),
                      pl.BlockSpec((1, 1, D, 2 * D_EXPERT), w_map),
                      pl.BlockSpec((1, 1, 1, 2 * D_EXPERT), w_map),
                      pl.BlockSpec((1, 1, D_EXPERT, D), w_map),
                      pl.BlockSpec((1, 1, 1, D), w_map)],
            out_specs=pl.BlockSpec((MOE_BM, D), lambda i, be, na: (i, 0)),
            scratch_shapes=[pltpu.VMEM((D, 2 * D_EXPERT), BF16),
                            pltpu.VMEM((D_EXPERT, D), BF16)]),
        out_shape=jax.ShapeDtypeStruct((n_slots, D), F32),
        compiler_params=pltpu.CompilerParams(dimension_semantics=("arbitrary",),
                                             vmem_limit_bytes=EXPERT_VMEM_LIMIT),
        name="moe_experts",
    )(blk_e, n_active, xs, w_gu, b_gu.reshape(DEPTH, N_EXPERTS, 1, 2 * D_EXPERT), w_down,
      b_down.reshape(DEPTH, N_EXPERTS, 1, D))


def _combine_kernel(dest_ref, gate_ref, x_ref, gt_ref, gfin_ref, yb_ref, o_ref, buf, sem, *,
                    tm, final_norm):
    def body(i, carry):
        for k in range(TOP_K):
            pltpu.make_async_copy(yb_ref.at[pl.ds(dest_ref[0, k, i], 1)], buf.at[k, pl.ds(i, 1)],
                                  sem).start(priority=k % 2)
        return carry

    lax.fori_loop(0, tm, body, 0, unroll=4)
    for k in range(TOP_K):
        pltpu.make_async_copy(yb_ref.at[pl.ds(0, tm)], buf.at[k], sem).wait()
    gates = gate_ref[0]
    y = buf[0] * gates[:, 0:1]
    for k in range(1, TOP_K):
        y = y + buf[k] * gates[:, k:k + 1]
    x = x_ref[0] + gt_ref[0] * y
    if final_norm:
        ms = jnp.mean(x * x, axis=-1, keepdims=True)
        x = x * lax.rsqrt(ms + EPS) * gfin_ref[...]
    o_ref[0] = x


def _combine_call(dest, gates, x, gt, g_final, yb, *, tm, final_norm):
    B, L, D = x.shape
    per_batch = gt.shape[0] > 1
    mod_map = (lambda b, i: (b, 0, 0)) if per_batch else (lambda b, i: (0, 0, 0))
    return pl.pallas_call(
        functools.partial(_combine_kernel, tm=tm, final_norm=final_norm),
        grid=(B, L // tm),
        in_specs=[pl.BlockSpec((1, 8, tm), lambda b, i: (b, 0, i), memory_space=pltpu.SMEM),
                  pl.BlockSpec((1, tm, 128), lambda b, i: (b, i, 0)),
                  pl.BlockSpec((1, tm, D), lambda b, i: (b, i, 0)),
                  pl.BlockSpec((1, 1, D), mod_map),
                  pl.BlockSpec((1, D), lambda b, i: (0, 0)),
                  pl.BlockSpec(memory_space=pl.ANY)],
        out_specs=pl.BlockSpec((1, tm, D), lambda b, i: (b, i, 0)),
        out_shape=jax.ShapeDtypeStruct((B, L, D), F32),
        scratch_shapes=[pltpu.VMEM((TOP_K, tm, D), F32), pltpu.SemaphoreType.DMA(())],
        compiler_params=_cparams(("arbitrary", "arbitrary")),
        name="moe_combine",
    )(dest, gates, x, gt, g_final.reshape(1, D), yb)


def _moe(streams, layer_idx, experts, g_final, final_norm):
    counts = streams[-1]["cnt"].reshape(N_EXPERTS).astype(jnp.int32)
    padded = (counts + MOE_BM - 1) // MOE_BM * MOE_BM
    cum_pad = jnp.cumsum(padded)
    pstart = cum_pad - padded
    total_tokens = sum(s["x"].shape[0] * s["x"].shape[1] for s in streams)
    nb = -(-total_tokens * TOP_K // MOE_BM) + N_EXPERTS
    blk_e = jnp.minimum(jnp.sum(cum_pad[None, :] <= jnp.arange(nb)[:, None] * MOE_BM, axis=1),
                        N_EXPERTS - 1).astype(jnp.int32)
    n_active = (cum_pad[-1] // MOE_BM).astype(jnp.int32).reshape(1)
    pstart = pstart.astype(jnp.int32)

    expert_ids = jnp.arange(N_EXPERTS, dtype=jnp.int32).reshape(N_EXPERTS, 1, 1, 1)
    dests = []
    for s in streams:
        seg = jnp.sum(jnp.where(s["idx"][None] == expert_ids, pstart.reshape(N_EXPERTS, 1, 1, 1), 0),
                      axis=0)
        dests.append(seg + s["rank"])
    xs = _dispatch_call(pstart, counts, padded.astype(jnp.int32), dests,
                        [s["hf"] for s in streams], nb * MOE_BM, tm=256)
    yb = _expert_call(layer_idx, blk_e, n_active, xs, experts["w_gu"], experts["b_gu"],
                      experts["w_down"], experts["b_down"])
    outs = []
    for s, dest in zip(streams, dests):
        outs.append(_combine_call(dest, s["gate"], s["x"], s["gt"], g_final, yb,
                                  tm=min(256, s["x"].shape[1]), final_norm=final_norm))
    return outs


def _layer(x, xc, c8, p, experts, layer_idx, update_ctx, tabs, g_final, final_norm):
    B, S, D = x.shape
    C = xc.shape[1]
    lam_init = 0.8 - 0.6 * math.exp(-0.3 * layer_idx)
    lam = (jnp.exp(jnp.sum(p["lam_q1"] * p["lam_k1"])) - jnp.exp(jnp.sum(p["lam_q2"] * p["lam_k2"]))
           + lam_init).reshape(1).astype(F32)

    mod8 = _mod_call(c8, p["w_mod"], p["b_mod"])
    mod = [m.reshape(B, 1, D) for m in jnp.split(mod8[:B], 6, axis=-1)]
    modc = [m.reshape(1, 1, D) for m in jnp.split(mod8[B:B + 1], 6, axis=-1)]
    sh1, sc1, gt1, sh2, sc2, gt2 = mod

    w_in = p["w_in"].astype(BF16)
    w_out = p["w_out"].astype(BF16)
    proj = _inproj_call(x, sc1, sh1, p["g_mix"], w_in, tabs["cos"], tabs["sin"], rope=True, tm=512)
    projc = _inproj_call(xc, modc[1], modc[0], p["g_mix"], w_in, tabs["cos"], tabs["sin"],
                         rope=False, tm=C)

    gsub_col = p["g_sub"].reshape(HEAD_DIM, 1)
    oa = _da_call(lam, proj, proj, projc, gsub_col, lam_init=lam_init, tq=256, tkc=1024)
    bias = _na_bias_tables(p["rpb"], S // GRID_W)
    od = _na_call(proj, projc, bias)

    wpool_bd = jax.scipy.linalg.block_diag(*[p["w_pool"][g] for g in range(4)]).astype(BF16)
    spool = p["s_pool"].reshape(1, GROUP_WIDTH)
    gsgu = p["g_sgu"].reshape(1, GROUP_WIDTH)
    wsgu_cat = jnp.transpose(p["w_sgu"], (1, 0, 2)).reshape(SGU_CHUNK, N_HEADS * SGU_CHUNK).astype(BF16)
    bsgu_full = jnp.repeat(p["b_sgu"].T, HEAD_DIM, axis=1)
    pool, sgu = _mixers_call(proj, wpool_bd, spool, gsgu, wsgu_cat, bsgu_full, tm=512)

    cnt0 = jnp.zeros((N_EXPERTS, 1), F32)
    wr3 = _router_weights(p["w_router"])
    x1, hf, idx, gate, rank, cnt = _outproj_call(
        cnt0, (oa, pool, sgu, od), w_out, x, gt1, p["g_ffn"], sc2, sh2,
        wr3, p["b_router"], tm=256)
    streams = [dict(x=x1, hf=hf, idx=idx, gate=gate, rank=rank, gt=gt2, cnt=cnt)]

    if update_ctx:
        oa_c = _da_call(lam, projc, None, projc, gsub_col, lam_init=lam_init,
                        tq=C, tkc=512)
        od_c = _ctx_attn_call(projc)
        pool_c, sgu_c = _mixers_call(projc, wpool_bd, spool, gsgu, wsgu_cat, bsgu_full, tm=C)
        xc1, hfc, idxc, gatec, rankc, cnt = _outproj_call(
            cnt, (oa_c, pool_c, sgu_c, od_c), w_out, xc, modc[2], p["g_ffn"], modc[4], modc[3],
            wr3, p["b_router"], tm=C)
        streams.append(dict(x=xc1, hf=hfc, idx=idxc, gate=gatec, rank=rankc, gt=modc[5], cnt=cnt))

    outs = _moe(streams, layer_idx, experts, g_final, final_norm)
    if update_ctx:
        return outs[0], outs[1]
    return outs[0], xc


def kernel(x, c, ctx, c_ctx, w_mod, b_mod, g_mix, g_ffn, w_in, w_out, lam_q1, lam_k1, lam_q2,
           lam_k2, g_sub, w_pool, s_pool, g_sgu, w_sgu, b_sgu, rpb, w_router, b_router, w_gu,
           b_gu, w_down, b_down, g_final):
    B, S, D = x.shape
    cos_tab, sin_tab = _rope_tables(S)
    tabs = dict(cos=cos_tab, sin=sin_tab)
    c8 = jnp.concatenate([c, c_ctx[None, :], jnp.zeros((8 - B - 1, D), F32)], axis=0)
    xc = ctx
    experts = dict(w_gu=w_gu, b_gu=b_gu, w_down=w_down, b_down=b_down)
    for l in range(DEPTH):
        p = dict(w_mod=w_mod[l], b_mod=b_mod[l], g_mix=g_mix[l], g_ffn=g_ffn[l], w_in=w_in[l],
                 w_out=w_out[l], lam_q1=lam_q1[l], lam_k1=lam_k1[l], lam_q2=lam_q2[l],
                 lam_k2=lam_k2[l], g_sub=g_sub[l], w_pool=w_pool[l], s_pool=s_pool[l],
                 g_sgu=g_sgu[l], w_sgu=w_sgu[l], b_sgu=b_sgu[l], rpb=rpb[l],
                 w_router=w_router[l], b_router=b_router[l])
        x, xc = _layer(x, xc, c8, p, experts, l, l < DEPTH - 1, tabs, g_final, l == DEPTH - 1)
    return x
```

```python
import functools
import math

import jax
import jax.numpy as jnp
import numpy as np
from jax import lax
from jax.experimental import pallas as pl
from jax.experimental.pallas import tpu as pltpu

F32 = jnp.float32
BF16 = jnp.bfloat16

D_MODEL = 1024
DEPTH = 2
GRID_W = 64
HEAD_DIM = 64
N_HEADS = 4
GROUP_WIDTH = N_HEADS * HEAD_DIM
PROJ_WIDTH = 9 * GROUP_WIDTH
DA_QK = HEAD_DIM // 2
ROPE_BASE = 10000.0
POOL_WINDOWS = (2, 4, 8, 16)
SGU_CHUNK = 128
WIN_R = 8
WIN_C = 16
N_EXPERTS = 32
TOP_K = 4
D_EXPERT = D_MODEL
SWIGLU_LIMIT = 7.0
SWIGLU_ALPHA = 1.702
EPS = 1e-6
NEG_INF = -1e30
LOG2_E = math.log2(math.e)

COL_AQ, COL_AK, COL_AV, COL_POOL, COL_SGU, COL_NQ, COL_NK, COL_NV = 0, 1, 2, 3, 4, 6, 7, 8

NA_QROWS = 4
NA_BAND = 12
POOL_HALO = 16
MOE_BM = 512
PAD_CHUNK = 32
VMEM_LIMIT = 48 * 1024 * 1024
EXPERT_VMEM_LIMIT = 56 * 1024 * 1024


def _cparams(sem):
    return pltpu.CompilerParams(dimension_semantics=sem, vmem_limit_bytes=VMEM_LIMIT)


def _lane_iota(n):
    return lax.broadcasted_iota(jnp.int32, (1, n), 1)


def _dot_nt(a, b):
    return lax.dot_general(a, b, (((1,), (1,)), ((), ())), preferred_element_type=F32)


def _mod_kernel(c_ref, w_ref, b_ref, o_ref):
    c = c_ref[...]
    s = c * (1.0 / (1.0 + jnp.exp(-c)))
    o_ref[...] = jnp.dot(s.astype(BF16), w_ref[...].astype(BF16),
                         preferred_element_type=F32) + b_ref[...]


def _mod_call(c8, w_mod, b_mod):
    n = w_mod.shape[1]
    tn = 1536
    return pl.pallas_call(
        _mod_kernel,
        grid=(n // tn,),
        in_specs=[pl.BlockSpec((8, D_MODEL), lambda j: (0, 0)),
                  pl.BlockSpec((D_MODEL, tn), lambda j: (0, j)),
                  pl.BlockSpec((1, tn), lambda j: (0, j))],
        out_specs=pl.BlockSpec((8, tn), lambda j: (0, j)),
        out_shape=jax.ShapeDtypeStruct((8, n), F32),
        compiler_params=_cparams(("arbitrary",)),
        name="adaln_mod",
    )(c8, w_mod, b_mod.reshape(1, n))


def _swap8(x):
    n = x.shape[-1]
    lane = _lane_iota(n)
    first_half = (lane & 15) < 8
    return jnp.where(first_half, pltpu.roll(x, n - 8, axis=1), pltpu.roll(x, 8, axis=1))


def _inproj_kernel(x_ref, sc_ref, sh_ref, g_ref, w_ref, cos_ref, sin_ref, o_ref, *, rope):
    x = x_ref[0]
    ms = jnp.mean(x * x, axis=-1, keepdims=True)
    y = x * lax.rsqrt(ms + EPS) * g_ref[...]
    h = (y * (1.0 + sc_ref[0]) + sh_ref[0]).astype(BF16)
    for j in range(PROJ_WIDTH // GROUP_WIDTH):
        cols = slice(j * GROUP_WIDTH, (j + 1) * GROUP_WIDTH)
        acc = jnp.dot(h, w_ref[:, cols], preferred_element_type=F32)
        if rope and j in (COL_AQ, COL_AK):
            acc = acc * cos_ref[...] + _swap8(acc) * sin_ref[...]
        if j == COL_AQ:
            acc = acc * (DA_QK ** -0.5 * LOG2_E)
        if j == COL_NQ:
            acc = acc * (HEAD_DIM ** -0.5)
        o_ref[0, :, cols] = acc.astype(o_ref.dtype)


def _inproj_call(x, sc, sh, g, w_bf16, cos_tab, sin_tab, *, rope, tm):
    B, L, D = x.shape
    per_batch = sc.shape[0] > 1
    mod_map = (lambda b, i: (b, 0, 0)) if per_batch else (lambda b, i: (0, 0, 0))
    tab_map = (lambda b, i: (i, 0)) if rope else (lambda b, i: (0, 0))
    return pl.pallas_call(
        functools.partial(_inproj_kernel, rope=rope),
        grid=(B, L // tm),
        in_specs=[pl.BlockSpec((1, tm, D), lambda b, i: (b, i, 0)),
                  pl.BlockSpec((1, 1, D), mod_map),
                  pl.BlockSpec((1, 1, D), mod_map),
                  pl.BlockSpec((1, D), lambda b, i: (0, 0)),
                  pl.BlockSpec((D, PROJ_WIDTH), lambda b, i: (0, 0)),
                  pl.BlockSpec((tm, GROUP_WIDTH), tab_map),
                  pl.BlockSpec((tm, GROUP_WIDTH), tab_map)],
        out_specs=pl.BlockSpec((1, tm, PROJ_WIDTH), lambda b, i: (b, i, 0)),
        out_shape=jax.ShapeDtypeStruct((B, L, PROJ_WIDTH), BF16),
        compiler_params=_cparams(("arbitrary", "arbitrary")),
        name="in_proj",
    )(x, sc, sh, g.reshape(1, D), w_bf16, cos_tab, sin_tab)


def _rope_tables(S):
    n = 8
    inv = ROPE_BASE ** (-jnp.arange(n, dtype=F32) / n)
    t = jnp.arange(S)
    row_pos = (t // GRID_W).astype(F32)
    col_pos = (t % GRID_W).astype(F32)
    parts_c, parts_s = [], []
    for pos in (row_pos, col_pos):
        ang = pos[:, None] * inv[None, :]
        c, s = jnp.cos(ang), jnp.sin(ang)
        parts_c += [c, c]
        parts_s += [-s, s]
    cos32 = jnp.concatenate(parts_c, axis=1)
    sin32 = jnp.concatenate(parts_s, axis=1)
    reps = GROUP_WIDTH // 32
    return jnp.tile(cos32, (1, reps)), jnp.tile(sin32, (1, reps))


def _da_kernel(lam_ref, *refs, n_lat, tkc, post_scale):
    if n_lat:
        q_ref, kl_ref, vlt_ref, kc_ref, vct_ref, gsub_ref, o_ref = refs[:7]
    else:
        q_ref, kc_ref, vct_ref, gsub_ref, o_ref = refs[:5]
    qs_scr, m_scr, l_scr, acc_scr = refs[-4:]
    q = q_ref[0]
    tq = q.shape[0]
    lane = _lane_iota(GROUP_WIDTH)
    for g in range(2 * N_HEADS):
        sel = (lane >= g * DA_QK) & (lane < (g + 1) * DA_QK)
        qs_scr[g * tq:(g + 1) * tq, :] = q * sel.astype(BF16)
    m_scr[...] = jnp.full(m_scr.shape, NEG_INF, F32)
    l_scr[...] = jnp.zeros(l_scr.shape, F32)
    acc_scr[...] = jnp.zeros(acc_scr.shape, F32)

    def step(k, vt):
        def scores(h):
            return _dot_nt(k, qs_scr[2 * h * tq:(2 * h + 2) * tq, :])

        st_next = scores(0)
        for h in range(N_HEADS):
            rows = slice(h * HEAD_DIM, (h + 1) * HEAD_DIM)
            cols = slice(2 * h * tq, (2 * h + 2) * tq)
            st = st_next
            if h + 1 < N_HEADS:
                st_next = scores(h + 1)
            m_old = m_scr[:, cols]
            m_new = jnp.maximum(m_old, jnp.max(st, axis=0, keepdims=True))
            alpha = jnp.exp2(m_old - m_new)
            p = jnp.exp2(st - m_new)
            l_scr[:, cols] = alpha * l_scr[:, cols] + jnp.sum(p, axis=0, keepdims=True)
            m_scr[:, cols] = m_new
            pv = jnp.dot(vt[rows, :], p.astype(BF16), preferred_element_type=F32)
            acc_scr[rows, :] = alpha * acc_scr[rows, :] + pv

    if n_lat:
        def body(c, carry):
            off = pl.multiple_of(c * tkc, tkc)
            step(kl_ref[0, pl.ds(off, tkc), :], vlt_ref[0, c])
            return carry
        lax.fori_loop(0, n_lat, body, 0)
    step(kc_ref[0], vct_ref[0, 0])

    lam = lam_ref[0]
    outs = []
    for h in range(N_HEADS):
        rows = slice(h * HEAD_DIM, (h + 1) * HEAD_DIM)
        cols = slice(2 * h * tq, (2 * h + 2) * tq)
        o = acc_scr[rows, :] / l_scr[:, cols]
        o_h = o[:, :tq] - lam * o[:, tq:]
        ms = jnp.mean(o_h * o_h, axis=0, keepdims=True)
        outs.append(o_h * lax.rsqrt(ms + EPS) * gsub_ref[...] * post_scale)
    o_ref[0] = jnp.concatenate(outs, axis=0).T.astype(o_ref.dtype)


def _values_t(proj, tkc):
    B, L, _ = proj.shape
    v = proj[:, :, COL_AV * GROUP_WIDTH:(COL_AV + 1) * GROUP_WIDTH]
    return jnp.transpose(v.reshape(B, L // tkc, tkc, GROUP_WIDTH), (0, 1, 3, 2))


def _da_call(lam, proj_q, proj_lat, proj_ctx, gsub_col, *, lam_init, tq, tkc):
    B, Lq, _ = proj_q.shape
    C = proj_ctx.shape[1]
    in_specs = [pl.BlockSpec(memory_space=pltpu.SMEM),
                pl.BlockSpec((1, tq, GROUP_WIDTH), lambda b, i: (b, i, COL_AQ))]
    args = [lam, proj_q]
    n_lat = 0
    if proj_lat is not None:
        S = proj_lat.shape[1]
        n_lat = S // tkc
        in_specs += [pl.BlockSpec((1, S, GROUP_WIDTH), lambda b, i: (b, 0, COL_AK)),
                     pl.BlockSpec((1, n_lat, GROUP_WIDTH, tkc), lambda b, i: (b, 0, 0, 0))]
        args += [proj_lat, _values_t(proj_lat, tkc)]
    in_specs += [pl.BlockSpec((1, C, GROUP_WIDTH), lambda b, i: (b, 0, COL_AK)),
                 pl.BlockSpec((1, 1, GROUP_WIDTH, C), lambda b, i: (b, 0, 0, 0)),
                 pl.BlockSpec((HEAD_DIM, 1), lambda b, i: (0, 0))]
    args += [proj_ctx, _values_t(proj_ctx, C), gsub_col]
    n_stack = 2 * N_HEADS * tq
    return pl.pallas_call(
        functools.partial(_da_kernel, n_lat=n_lat, tkc=tkc, post_scale=1.0 - lam_init),
        grid=(B, Lq // tq),
        in_specs=in_specs,
        out_specs=pl.BlockSpec((1, tq, GROUP_WIDTH), lambda b, i: (b, i, 0)),
        out_shape=jax.ShapeDtypeStruct((B, Lq, GROUP_WIDTH), BF16),
        scratch_shapes=[pltpu.VMEM((n_stack, GROUP_WIDTH), BF16),
                        pltpu.VMEM((1, n_stack), F32),
                        pltpu.VMEM((1, n_stack), F32),
                        pltpu.VMEM((GROUP_WIDTH, 2 * tq), F32)],
        compiler_params=_cparams(("arbitrary", "arbitrary")),
        name="diff_attn" if n_lat else "diff_attn_ctx",
    )(*args)


def _softmax_heads(q, sources, out_dtype):
    tq = q.shape[0]
    lane = _lane_iota(GROUP_WIDTH)
    out = jnp.zeros((tq, GROUP_WIDTH), F32)
    for h in range(N_HEADS):
        hm = (lane >= h * HEAD_DIM) & (lane < (h + 1) * HEAD_DIM)
        qm = q * hm.astype(BF16)
        scores = []
        for k, _, bias_fn in sources:
            s = _dot_nt(qm, k)
            if bias_fn is not None:
                s = s + bias_fn(h)
            scores.append(s)
        m = scores[0].max(axis=-1, keepdims=True)
        for s in scores[1:]:
            m = jnp.maximum(m, s.max(axis=-1, keepdims=True))
        l = jnp.zeros((tq, 1), F32)
        o = jnp.zeros((tq, GROUP_WIDTH), F32)
        for s, (_, v, _) in zip(scores, sources):
            p = jnp.exp(s - m)
            l = l + jnp.sum(p, axis=-1, keepdims=True)
            o = o + jnp.dot(p.astype(BF16), v, preferred_element_type=F32)
        out = jnp.where(hm, o / l, out)
    return out.astype(out_dtype)


def _na_kernel(q_ref, kn_ref, vn_ref, kc_ref, vc_ref, bias_ref, o_ref, *, n_groups):
    g = pl.program_id(1)
    tq = NA_QROWS * GRID_W
    band = NA_BAND * GRID_W
    start = pl.multiple_of(jnp.clip(g - 1, 0, n_groups - NA_BAND // NA_QROWS) * tq, tq)
    kb = kn_ref[0, pl.ds(start, band), :]
    vb = vn_ref[0, pl.ds(start, band), :]
    sources = [(kb, vb, lambda h: bias_ref[0, h]), (kc_ref[0], vc_ref[0], None)]
    o_ref[0] = _softmax_heads(q_ref[0], sources, o_ref.dtype)


def _na_call(proj, proj_ctx, bias):
    B, S, _ = proj.shape
    C = proj_ctx.shape[1]
    tq = NA_QROWS * GRID_W
    band = NA_BAND * GRID_W
    n_groups = S // tq

    def bias_map(b, g):
        return (jnp.where(g == 0, 0, jnp.where(g == n_groups - 1, 2, 1)), 0, 0, 0)

    return pl.pallas_call(
        functools.partial(_na_kernel, n_groups=n_groups),
        grid=(B, n_groups),
        in_specs=[pl.BlockSpec((1, tq, GROUP_WIDTH), lambda b, g: (b, g, COL_NQ)),
                  pl.BlockSpec((1, S, GROUP_WIDTH), lambda b, g: (b, 0, COL_NK)),
                  pl.BlockSpec((1, S, GROUP_WIDTH), lambda b, g: (b, 0, COL_NV)),
                  pl.BlockSpec((1, C, GROUP_WIDTH), lambda b, g: (b, 0, COL_NK)),
                  pl.BlockSpec((1, C, GROUP_WIDTH), lambda b, g: (b, 0, COL_NV)),
                  pl.BlockSpec((1, N_HEADS, tq, band), bias_map)],
        out_specs=pl.BlockSpec((1, tq, GROUP_WIDTH), lambda b, g: (b, g, 0)),
        out_shape=jax.ShapeDtypeStruct((B, S, GROUP_WIDTH), BF16),
        compiler_params=_cparams(("arbitrary", "arbitrary")),
        name="nbr_attn",
    )(proj, proj, proj, proj_ctx, proj_ctx, bias)


def _na_bias_tables(rpb, rows):
    n_groups = rows // NA_QROWS
    n_rr, n_rc = 2 * WIN_R - 1, 2 * WIN_C - 1
    c = np.arange(GRID_W)[:, None]
    kc = np.arange(GRID_W)[None, :]
    col_lo = np.clip(c - WIN_C // 2, 0, GRID_W - WIN_C)
    col_ok = (kc >= col_lo) & (kc < col_lo + WIN_C)
    rel_c = np.clip(kc - c, -(WIN_C - 1), WIN_C - 1) + (WIN_C - 1)
    sel_c = (rel_c[..., None] == np.arange(n_rc)).astype(np.float32)
    sel_r, ok = [], []
    for g in (0, 1, n_groups - 1):
        r0 = g * NA_QROWS
        start = min(max(r0 - WIN_R // 2, 0), rows - NA_BAND)
        r = r0 + np.arange(NA_QROWS)[:, None]
        kr = start + np.arange(NA_BAND)[None, :]
        row_lo = np.clip(r - WIN_R // 2, 0, rows - WIN_R)
        row_ok = (kr >= row_lo) & (kr < row_lo + WIN_R)
        rel_r = kr - r + (WIN_R - 1)
        sel_r.append((rel_r[..., None] == np.arange(n_rr)).astype(np.float32))
        ok.append(row_ok[:, None, :, None] & col_ok[None, :, None, :])
    sel_r = jnp.asarray(np.stack(sel_r))
    ok = np.stack(ok).reshape(3, 1, NA_QROWS * GRID_W, NA_BAND * GRID_W)
    vals = jnp.einsum("vxka,hab,cyb->vhxcky", sel_r, rpb.astype(F32), jnp.asarray(sel_c),
                      precision=lax.Precision.HIGHEST)
    vals = vals.reshape(3, N_HEADS, NA_QROWS * GRID_W, NA_BAND * GRID_W)
    return jnp.where(jnp.asarray(ok), vals, NEG_INF)


def _ctx_attn_kernel(q_ref, k_ref, v_ref, o_ref):
    o_ref[0] = _softmax_heads(q_ref[0], [(k_ref[0], v_ref[0], None)], o_ref.dtype)


def _ctx_attn_call(proj_ctx):
    B, C, _ = proj_ctx.shape
    return pl.pallas_call(
        _ctx_attn_kernel,
        grid=(B,),
        in_specs=[pl.BlockSpec((1, C, GROUP_WIDTH), lambda b: (b, 0, COL_NQ)),
                  pl.BlockSpec((1, C, GROUP_WIDTH), lambda b: (b, 0, COL_NK)),
                  pl.BlockSpec((1, C, GROUP_WIDTH), lambda b: (b, 0, COL_NV))],
        out_specs=pl.BlockSpec((1, C, GROUP_WIDTH), lambda b: (b, 0, 0)),
        out_shape=jax.ShapeDtypeStruct((B, C, GROUP_WIDTH), BF16),
        compiler_params=_cparams(("arbitrary",)),
        name="ctx_attn",
    )(proj_ctx, proj_ctx, proj_ctx)


def _gelu_tanh(x):
    return 0.5 * x * (1.0 + jnp.tanh(math.sqrt(2.0 / math.pi) * (x + 0.044715 * (x * x * x))))


def _mixers_kernel(pc_ref, pp_ref, pn_ref, sg_ref, wpool_ref, spool_ref, gsgu_ref, wsgu_ref,
                   bsgu_ref, pool_o, sgu_o, *, tm, seq_len):
    i = pl.program_id(1)
    n_tiles = seq_len // tm
    lane = _lane_iota(GROUP_WIDTH)
    grp = lax.shift_right_logical(lane, 6)

    def by_group(vals):
        return jnp.where(grp == 0, vals[0], jnp.where(grp == 1, vals[1],
                                                       jnp.where(grp == 2, vals[2], vals[3])))

    cur = pc_ref[0].astype(F32)
    prev = jnp.where(i > 0, pp_ref[0].astype(F32), 0.0)
    nxt = jnp.where(i < n_tiles - 1, pn_ref[0].astype(F32), 0.0)
    ext = jnp.concatenate([prev, cur, nxt], axis=0)
    n = tm + 2 * POOL_HALO

    def ahead(x, d):
        return pltpu.roll(x, n - d, axis=0)

    def behind(x, d):
        return pltpu.roll(x, d, axis=0)

    f2 = ext + ahead(ext, 1)
    f4 = f2 + ahead(f2, 2)
    f8 = f4 + ahead(f4, 4)
    f16 = f8 + ahead(f8, 8)
    sums = by_group([behind(f2, 1), behind(f4, 2), behind(f8, 4), behind(f16, 8)])
    sums = sums[POOL_HALO:POOL_HALO + tm]
    t = i * tm + lax.broadcasted_iota(jnp.int32, (tm, 1), 0)
    cnts = []
    for w in POOL_WINDOWS:
        lo = jnp.maximum(t - w // 2, 0)
        hi = jnp.minimum(t + (w - w // 2 - 1), seq_len - 1)
        cnts.append((hi - lo + 1).astype(F32))
    cnt = by_group(cnts)
    diff = (sums / cnt - cur).astype(BF16)
    pool = jnp.dot(diff, wpool_ref[...], preferred_element_type=F32) * spool_ref[...]
    pool_o[0] = pool.astype(pool_o.dtype)

    z = _gelu_tanh(sg_ref[0].astype(F32))
    u = z[:, :GROUP_WIDTH]
    v = z[:, GROUP_WIDTH:]
    v = v * lax.rsqrt(jnp.mean(v * v, axis=-1, keepdims=True) + EPS) * gsgu_ref[...]
    v = v.astype(BF16)
    hmasks = [(grp == h).astype(BF16) for h in range(N_HEADS)]
    for c in range(tm // SGU_CHUNK):
        rows = slice(c * SGU_CHUNK, (c + 1) * SGU_CHUNK)
        vc = v[rows]
        vstack = jnp.concatenate([vc * hm for hm in hmasks], axis=0)
        s = jnp.dot(wsgu_ref[...], vstack, preferred_element_type=F32) + bsgu_ref[...]
        sgu_o[0, rows, :] = (u[rows] * s).astype(sgu_o.dtype)


def _mixers_call(proj, wpool_bd, spool, gsgu, wsgu_cat, bsgu_full, *, tm):
    B, L, _ = proj.shape
    hb = tm // POOL_HALO
    n_halo = L // POOL_HALO
    return pl.pallas_call(
        functools.partial(_mixers_kernel, tm=tm, seq_len=L),
        grid=(B, L // tm),
        in_specs=[pl.BlockSpec((1, tm, GROUP_WIDTH), lambda b, i: (b, i, COL_POOL)),
                  pl.BlockSpec((1, POOL_HALO, GROUP_WIDTH),
                               lambda b, i: (b, jnp.maximum(i * hb - 1, 0), COL_POOL)),
                  pl.BlockSpec((1, POOL_HALO, GROUP_WIDTH),
                               lambda b, i: (b, jnp.minimum((i + 1) * hb, n_halo - 1), COL_POOL)),
                  pl.BlockSpec((1, tm, 2 * GROUP_WIDTH), lambda b, i: (b, i, COL_SGU // 2)),
                  pl.BlockSpec((GROUP_WIDTH, GROUP_WIDTH), lambda b, i: (0, 0)),
                  pl.BlockSpec((1, GROUP_WIDTH), lambda b, i: (0, 0)),
                  pl.BlockSpec((1, GROUP_WIDTH), lambda b, i: (0, 0)),
                  pl.BlockSpec((SGU_CHUNK, N_HEADS * SGU_CHUNK), lambda b, i: (0, 0)),
                  pl.BlockSpec((SGU_CHUNK, GROUP_WIDTH), lambda b, i: (0, 0))],
        out_specs=[pl.BlockSpec((1, tm, GROUP_WIDTH), lambda b, i: (b, i, 0)),
                   pl.BlockSpec((1, tm, GROUP_WIDTH), lambda b, i: (b, i, 0))],
        out_shape=[jax.ShapeDtypeStruct((B, L, GROUP_WIDTH), BF16),
                   jax.ShapeDtypeStruct((B, L, GROUP_WIDTH), BF16)],
        compiler_params=_cparams(("arbitrary", "arbitrary")),
        name="pool_sgu",
    )(proj, proj, proj, proj, wpool_bd, spool, gsgu, wsgu_cat, bsgu_full)


def _outproj_kernel(cnt0_ref, p0, p1, p2, p3, w_ref, x_ref, gt_ref, g_ref, sc_ref, sh_ref,
                    wr3_ref, br_ref, xo_ref, hf_ref, idx_ref, gate_ref, rank_ref, cnt_ref, cnt_scr):
    first = (pl.program_id(0) == 0) & (pl.program_id(1) == 0)

    @pl.when(first)
    def _():
        cnt_scr[...] = cnt0_ref[...]

    acc = None
    for j, p in enumerate((p0, p1, p2, p3)):
        part = jnp.dot(p[0], w_ref[j * GROUP_WIDTH:(j + 1) * GROUP_WIDTH, :],
                       preferred_element_type=F32)
        acc = part if acc is None else acc + part
    x = x_ref[0] + gt_ref[0] * acc
    xo_ref[0] = x
    ms = jnp.mean(x * x, axis=-1, keepdims=True)
    hf = (x * lax.rsqrt(ms + EPS) * g_ref[...]) * (1.0 + sc_ref[0]) + sh_ref[0]
    hf_ref[0] = hf

    tm = x.shape[0]
    hi = hf.astype(BF16)
    lo = (hf - hi.astype(F32)).astype(BF16)
    lg = jnp.dot(jnp.concatenate([hi, hi, lo], axis=1), wr3_ref[...], preferred_element_type=F32)
    lt = lg.T[:N_EXPERTS] + br_ref[...]
    row_e = lax.broadcasted_iota(jnp.int32, (N_EXPERTS, tm), 0).astype(F32)
    vals, idxs = [], []
    l = lt
    for _ in range(TOP_K):
        mx = jnp.max(l, axis=0, keepdims=True)
        ik = jnp.min(jnp.where(l == mx, row_e, float(N_EXPERTS)), axis=0, keepdims=True)
        vals.append(mx)
        idxs.append(ik)
        l = jnp.where(row_e == ik, -jnp.inf, l)
    exps = [jnp.exp(v - vals[0]) for v in vals]
    denom = exps[0] + exps[1] + exps[2] + exps[3]
    gates = [e / denom for e in exps]

    onehot = jnp.zeros((N_EXPERTS, tm), F32)
    for ik in idxs:
        onehot = onehot + (row_e == ik).astype(F32)
    ri = lax.broadcasted_iota(jnp.int32, (tm, tm), 0)
    ci = lax.broadcasted_iota(jnp.int32, (tm, tm), 1)
    earlier = (ri < ci).astype(BF16)
    before = jnp.dot(onehot.astype(BF16), earlier, preferred_element_type=F32) + cnt_scr[...]
    ranks = [jnp.sum(jnp.where(row_e == ik, before, 0.0), axis=0, keepdims=True) for ik in idxs]
    cnt_scr[...] = cnt_scr[...] + jnp.sum(onehot, axis=1, keepdims=True)
    cnt_ref[...] = cnt_scr[...]

    def pack_rows(rows, n):
        r = lax.broadcasted_iota(jnp.int32, (n, tm), 0)
        o = jnp.zeros((n, tm), F32)
        for k, rv in enumerate(rows):
            o = jnp.where(r == k, rv, o)
        return o

    idx_ref[0] = pack_rows(idxs, 8).astype(jnp.int32)
    rank_ref[0] = pack_rows(ranks, 8).astype(jnp.int32)
    gate_ref[0] = pack_rows(gates, 128).T


def _router_weights(w_router):
    whi = w_router.astype(BF16)
    wlo = (w_router - whi.astype(F32)).astype(BF16)
    w3 = jnp.concatenate([whi, wlo, whi], axis=0)
    return jnp.pad(w3, ((0, 0), (0, 128 - N_EXPERTS)))


def _outproj_call(cnt0, parts, w_out_bf16, x, gt, g, sc, sh, wr3, b_router, *, tm):
    B, L, D = x.shape
    per_batch = gt.shape[0] > 1
    mod_map = (lambda b, i: (b, 0, 0)) if per_batch else (lambda b, i: (0, 0, 0))
    part_spec = pl.BlockSpec((1, tm, GROUP_WIDTH), lambda b, i: (b, i, 0))
    row_spec = pl.BlockSpec((1, tm, D), lambda b, i: (b, i, 0))
    gate_spec = pl.BlockSpec((1, tm, 128), lambda b, i: (b, i, 0))
    slot_spec = pl.BlockSpec((1, 8, tm), lambda b, i: (b, 0, i))
    cnt_spec = pl.BlockSpec((N_EXPERTS, 1), lambda b, i: (0, 0))
    return pl.pallas_call(
        _outproj_kernel,
        grid=(B, L // tm),
        in_specs=[cnt_spec, part_spec, part_spec, part_spec, part_spec,
                  pl.BlockSpec((D, D), lambda b, i: (0, 0)),
                  row_spec,
                  pl.BlockSpec((1, 1, D), mod_map),
                  pl.BlockSpec((1, D), lambda b, i: (0, 0)),
                  pl.BlockSpec((1, 1, D), mod_map),
                  pl.BlockSpec((1, 1, D), mod_map),
                  pl.BlockSpec((3 * D, 128), lambda b, i: (0, 0)),
                  cnt_spec],
        out_specs=[row_spec, row_spec, slot_spec, gate_spec, slot_spec, cnt_spec],
        out_shape=[jax.ShapeDtypeStruct((B, L, D), F32),
                   jax.ShapeDtypeStruct((B, L, D), F32),
                   jax.ShapeDtypeStruct((B, 8, L), jnp.int32),
                   jax.ShapeDtypeStruct((B, L, 128), F32),
                   jax.ShapeDtypeStruct((B, 8, L), jnp.int32),
                   jax.ShapeDtypeStruct((N_EXPERTS, 1), F32)],
        scratch_shapes=[pltpu.VMEM((N_EXPERTS, 1), F32)],
        compiler_params=_cparams(("arbitrary", "arbitrary")),
        name="out_proj_router",
    )(cnt0, *parts, w_out_bf16, x, gt, g.reshape(1, D), sc, sh, wr3,
      b_router.reshape(N_EXPERTS, 1))


def _dispatch_kernel(ps_ref, cnt_ref, pad_ref, *refs, tm, tiles):
    n_streams = len(tiles)
    dest_refs = refs[:n_streams]
    hf_refs = refs[n_streams:2 * n_streams]
    xs_out, zbuf, sem, zsem = refs[2 * n_streams:]
    j = pl.program_id(0)
    n_blocks = xs_out.shape[0] // MOE_BM

    @pl.when(j == 0)
    def _():
        zbuf[...] = jnp.zeros(zbuf.shape, zbuf.dtype)

        def zero_row_copy(row):
            return pltpu.make_async_copy(zbuf.at[pl.ds(0, 1)], xs_out.at[pl.ds(row, 1)], zsem)

        def zero_block_copy(blk):
            row = pl.multiple_of(blk * MOE_BM, MOE_BM)
            return pltpu.make_async_copy(zbuf, xs_out.at[pl.ds(row, MOE_BM)], zsem)

        def zero_chunk_copy(row):
            return pltpu.make_async_copy(zbuf.at[pl.ds(0, PAD_CHUNK)],
                                         xs_out.at[pl.ds(row, PAD_CHUNK)], zsem)

        def per_expert(e, totals):
            base = ps_ref[e] + cnt_ref[e]
            n = pad_ref[e] - cnt_ref[e]
            n_chunks = n // PAD_CHUNK
            n_single = n - n_chunks * PAD_CHUNK

            def per_row(r, carry):
                zero_row_copy(base + r).start()
                return carry

            def per_chunk(c, carry):
                row = ps_ref[e] + pad_ref[e] - (n_chunks - c) * PAD_CHUNK
                zero_chunk_copy(pl.multiple_of(row, PAD_CHUNK)).start()
                return carry

            lax.fori_loop(0, n_single, per_row, 0)
            lax.fori_loop(0, n_chunks, per_chunk, 0)
            return totals[0] + n_single, totals[1] + n_chunks

        n_rows, n_chunks_total = lax.fori_loop(0, N_EXPERTS, per_expert, (0, 0))
        first_unused = (ps_ref[N_EXPERTS - 1] + pad_ref[N_EXPERTS - 1]) // MOE_BM

        def per_block(blk, carry):
            zero_block_copy(blk).start()
            return carry

        lax.fori_loop(first_unused, n_blocks, per_block, 0)

        def drain_row(r, carry):
            zero_row_copy(0).wait()
            return carry

        def drain_block(blk, carry):
            zero_block_copy(0).wait()
            return carry

        def drain_chunk(c, carry):
            zero_chunk_copy(0).wait()
            return carry

        lax.fori_loop(0, n_rows, drain_row, 0)
        lax.fori_loop(0, n_chunks_total, drain_chunk, 0)
        lax.fori_loop(first_unused, n_blocks, drain_block, 0)

    first_tile = 0
    for dest_ref, hf_ref, n_tiles in zip(dest_refs, hf_refs, tiles):
        @pl.when((j >= first_tile) & (j < first_tile + n_tiles))
        def _(dest_ref=dest_ref, hf_ref=hf_ref):
            def body(i, carry):
                for k in range(TOP_K):
                    pltpu.make_async_copy(hf_ref.at[0, pl.ds(i, 1)],
                                          xs_out.at[pl.ds(dest_ref[0, k, i], 1)],
                                          sem).start(priority=k % 2)
                return carry

            lax.fori_loop(0, tm, body, 0, unroll=4)
            for k in range(TOP_K):
                pltpu.make_async_copy(hf_ref.at[0], xs_out.at[pl.ds(0, tm)], sem).wait()
        first_tile += n_tiles


def _dispatch_call(pstart, counts, padded, dests, hfs, n_slots, *, tm):
    D = hfs[0].shape[-1]
    tiles, dest_specs, hf_specs = [], [], []
    first_tile = 0
    for hf in hfs:
        B, L, _ = hf.shape
        n_l = L // tm
        n_tiles = B * n_l

        def tile_of(j, first=first_tile, n=n_tiles):
            return jnp.clip(j - first, 0, n - 1)

        dest_specs.append(pl.BlockSpec(
            (1, 8, tm), lambda j, *_, t=tile_of, n_l=n_l: (t(j) // n_l, 0, t(j) % n_l),
            memory_space=pltpu.SMEM))
        hf_specs.append(pl.BlockSpec(
            (1, tm, D), lambda j, *_, t=tile_of, n_l=n_l: (t(j) // n_l, t(j) % n_l, 0)))
        tiles.append(n_tiles)
        first_tile += n_tiles
    return pl.pallas_call(
        functools.partial(_dispatch_kernel, tm=tm, tiles=tuple(tiles)),
        grid_spec=pltpu.PrefetchScalarGridSpec(
            num_scalar_prefetch=3, grid=(first_tile,),
            in_specs=dest_specs + hf_specs,
            out_specs=pl.BlockSpec(memory_space=pl.ANY),
            scratch_shapes=[pltpu.VMEM((MOE_BM, D), F32), pltpu.SemaphoreType.DMA(()),
                            pltpu.SemaphoreType.DMA(())]),
        out_shape=jax.ShapeDtypeStruct((n_slots, D), F32),
        compiler_params=_cparams(("arbitrary",)),
        name="moe_dispatch",
    )(pstart, counts, padded, *dests, *hfs)


def _expert_kernel(be_ref, na_ref, nx_ref, xs_ref, wgu_hbm, bgu_ref, wd_hbm, bd_ref, y_ref,
                   wgu_f32, wd_f32, wgu_bf, wd_bf, sems, *, layer):
    i = pl.program_id(0)
    active = i < na_ref[0]
    e = be_ref[i]
    new_expert = (i == 0) | (e != be_ref[jnp.maximum(i - 1, 0)])

    def weight_copies(expert):
        return (pltpu.make_async_copy(wgu_hbm.at[layer, expert], wgu_f32, sems.at[0]),
                pltpu.make_async_copy(wd_hbm.at[layer, expert], wd_f32, sems.at[1]))

    @pl.when(active & (i == 0))
    def _():
        for cp in weight_copies(e):
            cp.start()

    @pl.when(active & new_expert)
    def _():
        for cp in weight_copies(e):
            cp.wait()
        wgu_bf[...] = wgu_f32[...].astype(BF16)
        wd_bf[...] = wd_f32[...].astype(BF16)

        @pl.when(nx_ref[e] >= 0)
        def _():
            for cp in weight_copies(nx_ref[e]):
                cp.start()

    @pl.when(active)
    def _():
        xb = xs_ref[...].astype(BF16)
        gu = jnp.dot(xb, wgu_bf[...], preferred_element_type=F32) + bgu_ref[0, 0]
        gt = jnp.minimum(gu[:, :D_EXPERT], SWIGLU_LIMIT)
        up = jnp.clip(gu[:, D_EXPERT:], -SWIGLU_LIMIT, SWIGLU_LIMIT)
        act = (up + 1.0) * gt * (1.0 / (1.0 + jnp.exp(-SWIGLU_ALPHA * gt)))
        y_ref[...] = jnp.dot(act.astype(BF16), wd_bf[...], preferred_element_type=F32) + bd_ref[0, 0]

    @pl.when(jnp.logical_not(active))
    def _():
        y_ref[...] = jnp.zeros(y_ref.shape, y_ref.dtype)


def _expert_call(layer, blk_e, n_active, next_e, xs, w_gu, b_gu, w_down, b_down):
    n_slots, D = xs.shape
    nb = n_slots // MOE_BM

    def row_map(i, be, na, nx):
        return (jnp.minimum(i, jnp.maximum(na[0] - 1, 0)), 0)

    def b_map(i, be, na, nx):
        return (layer, be[jnp.minimum(i, jnp.maximum(na[0] - 1, 0))], 0, 0)

    return pl.pallas_call(
        functools.partial(_expert_kernel, layer=layer),
        grid_spec=pltpu.PrefetchScalarGridSpec(
            num_scalar_prefetch=3,
            grid=(nb,),
            in_specs=[pl.BlockSpec((MOE_BM, D), row_map),
                      pl.BlockSpec(memory_space=pl.ANY),
                      pl.BlockSpec((1, 1, 1, 2 * D_EXPERT), b_map),
                      pl.BlockSpec(memory_space=pl.ANY),
                      pl.BlockSpec((1, 1, 1, D), b_map)],
            out_specs=pl.BlockSpec((MOE_BM, D), lambda i, be, na, nx: (i, 0)),
            scratch_shapes=[pltpu.VMEM((D, 2 * D_EXPERT), F32),
                            pltpu.VMEM((D_EXPERT, D), F32),
                            pltpu.VMEM((D, 2 * D_EXPERT), BF16),
                            pltpu.VMEM((D_EXPERT, D), BF16),
                            pltpu.SemaphoreType.DMA((2,))]),
        out_shape=jax.ShapeDtypeStruct((n_slots, D), F32),
        compiler_params=pltpu.CompilerParams(dimension_semantics=("arbitrary",),
                                             vmem_limit_bytes=EXPERT_VMEM_LIMIT),
        name="moe_experts",
    )(blk_e, n_active, next_e, xs, w_gu, b_gu.reshape(DEPTH, N_EXPERTS, 1, 2 * D_EXPERT), w_down,
      b_down.reshape(DEPTH, N_EXPERTS, 1, D))


def _combine_kernel(dest_ref, gate_ref, x_ref, gt_ref, gfin_ref, yb_ref, o_ref, buf, sem, *,
                    tm, final_norm):
    def body(i, carry):
        for k in range(TOP_K):
            pltpu.make_async_copy(yb_ref.at[pl.ds(dest_ref[0, k, i], 1)], buf.at[k, pl.ds(i, 1)],
                                  sem).start(priority=k % 2)
        return carry

    lax.fori_loop(0, tm, body, 0, unroll=4)
    for k in range(TOP_K):
        pltpu.make_async_copy(yb_ref.at[pl.ds(0, tm)], buf.at[k], sem).wait()
    gates = gate_ref[0]
    y = buf[0] * gates[:, 0:1]
    for k in range(1, TOP_K):
        y = y + buf[k] * gates[:, k:k + 1]
    x = x_ref[0] + gt_ref[0] * y
    if final_norm:
        ms = jnp.mean(x * x, axis=-1, keepdims=True)
        x = x * lax.rsqrt(ms + EPS) * gfin_ref[...]
    o_ref[0] = x


def _combine_call(dest, gates, x, gt, g_final, yb, *, tm, final_norm):
    B, L, D = x.shape
    per_batch = gt.shape[0] > 1
    mod_map = (lambda b, i: (b, 0, 0)) if per_batch else (lambda b, i: (0, 0, 0))
    return pl.pallas_call(
        functools.partial(_combine_kernel, tm=tm, final_norm=final_norm),
        grid=(B, L // tm),
        in_specs=[pl.BlockSpec((1, 8, tm), lambda b, i: (b, 0, i), memory_space=pltpu.SMEM),
                  pl.BlockSpec((1, tm, 128), lambda b, i: (b, i, 0)),
                  pl.BlockSpec((1, tm, D), lambda b, i: (b, i, 0)),
                  pl.BlockSpec((1, 1, D), mod_map),
                  pl.BlockSpec((1, D), lambda b, i: (0, 0)),
                  pl.BlockSpec(memory_space=pl.ANY)],
        out_specs=pl.BlockSpec((1, tm, D), lambda b, i: (b, i, 0)),
        out_shape=jax.ShapeDtypeStruct((B, L, D), F32),
        scratch_shapes=[pltpu.VMEM((TOP_K, tm, D), F32), pltpu.SemaphoreType.DMA(())],
        compiler_params=_cparams(("arbitrary", "arbitrary")),
        name="moe_combine",
    )(dest, gates, x, gt, g_final.reshape(1, D), yb)


def _moe(streams, layer_idx, experts, g_final, final_norm):
    counts = streams[-1]["cnt"].reshape(N_EXPERTS).astype(jnp.int32)
    padded = (counts + MOE_BM - 1) // MOE_BM * MOE_BM
    cum_pad = jnp.cumsum(padded)
    pstart = cum_pad - padded
    total_tokens = sum(s["x"].shape[0] * s["x"].shape[1] for s in streams)
    nb = -(-total_tokens * TOP_K // MOE_BM) + N_EXPERTS
    blk_e = jnp.minimum(jnp.sum(cum_pad[None, :] <= jnp.arange(nb)[:, None] * MOE_BM, axis=1),
                        N_EXPERTS - 1).astype(jnp.int32)
    n_active = (cum_pad[-1] // MOE_BM).astype(jnp.int32).reshape(1)
    pstart = pstart.astype(jnp.int32)
    owner = jnp.where(padded > 0, jnp.arange(N_EXPERTS), N_EXPERTS)
    later = jnp.concatenate([lax.cummin(owner, reverse=True)[1:], jnp.array([N_EXPERTS])])
    next_e = jnp.where(later < N_EXPERTS, later, -1).astype(jnp.int32)

    expert_ids = jnp.arange(N_EXPERTS, dtype=jnp.int32).reshape(N_EXPERTS, 1, 1, 1)
    dests = []
    for s in streams:
        seg = jnp.sum(jnp.where(s["idx"][None] == expert_ids, pstart.reshape(N_EXPERTS, 1, 1, 1), 0),
                      axis=0)
        dests.append(seg + s["rank"])
    xs = _dispatch_call(pstart, counts, padded.astype(jnp.int32), dests,
                        [s["hf"] for s in streams], nb * MOE_BM, tm=256)
    yb = _expert_call(layer_idx, blk_e, n_active, next_e, xs, experts["w_gu"], experts["b_gu"],
                      experts["w_down"], experts["b_down"])
    outs = []
    for s, dest in zip(streams, dests):
        outs.append(_combine_call(dest, s["gate"], s["x"], s["gt"], g_final, yb,
                                  tm=min(256, s["x"].shape[1]), final_norm=final_norm))
    return outs


def _layer(x, xc, c8, p, experts, layer_idx, update_ctx, tabs, g_final, final_norm):
    B, S, D = x.shape
    C = xc.shape[1]
    lam_init = 0.8 - 0.6 * math.exp(-0.3 * layer_idx)
    lam = (jnp.exp(jnp.sum(p["lam_q1"] * p["lam_k1"])) - jnp.exp(jnp.sum(p["lam_q2"] * p["lam_k2"]))
           + lam_init).reshape(1).astype(F32)

    mod8 = _mod_call(c8, p["w_mod"], p["b_mod"])
    mod = [m.reshape(B, 1, D) for m in jnp.split(mod8[:B], 6, axis=-1)]
    modc = [m.reshape(1, 1, D) for m in jnp.split(mod8[B:B + 1], 6, axis=-1)]
    sh1, sc1, gt1, sh2, sc2, gt2 = mod

    w_in = p["w_in"].astype(BF16)
    w_out = p["w_out"].astype(BF16)
    proj = _inproj_call(x, sc1, sh1, p["g_mix"], w_in, tabs["cos"], tabs["sin"], rope=True, tm=512)
    projc = _inproj_call(xc, modc[1], modc[0], p["g_mix"], w_in, tabs["cos"], tabs["sin"],
                         rope=False, tm=C)

    gsub_col = p["g_sub"].reshape(HEAD_DIM, 1)
    oa = _da_call(lam, proj, proj, projc, gsub_col, lam_init=lam_init, tq=256, tkc=1024)
    bias = _na_bias_tables(p["rpb"], S // GRID_W)
    od = _na_call(proj, projc, bias)

    wpool_bd = jax.scipy.linalg.block_diag(*[p["w_pool"][g] for g in range(4)]).astype(BF16)
    spool = p["s_pool"].reshape(1, GROUP_WIDTH)
    gsgu = p["g_sgu"].reshape(1, GROUP_WIDTH)
    wsgu_cat = jnp.transpose(p["w_sgu"], (1, 0, 2)).reshape(SGU_CHUNK, N_HEADS * SGU_CHUNK).astype(BF16)
    bsgu_full = jnp.repeat(p["b_sgu"].T, HEAD_DIM, axis=1)
    pool, sgu = _mixers_call(proj, wpool_bd, spool, gsgu, wsgu_cat, bsgu_full, tm=512)

    cnt0 = jnp.zeros((N_EXPERTS, 1), F32)
    wr3 = _router_weights(p["w_router"])
    x1, hf, idx, gate, rank, cnt = _outproj_call(
        cnt0, (oa, pool, sgu, od), w_out, x, gt1, p["g_ffn"], sc2, sh2,
        wr3, p["b_router"], tm=256)
    streams = [dict(x=x1, hf=hf, idx=idx, gate=gate, rank=rank, gt=gt2, cnt=cnt)]

    if update_ctx:
        oa_c = _da_call(lam, projc, None, projc, gsub_col, lam_init=lam_init,
                        tq=C, tkc=512)
        od_c = _ctx_attn_call(projc)
        pool_c, sgu_c = _mixers_call(projc, wpool_bd, spool, gsgu, wsgu_cat, bsgu_full, tm=C)
        xc1, hfc, idxc, gatec, rankc, cnt = _outproj_call(
            cnt, (oa_c, pool_c, sgu_c, od_c), w_out, xc, modc[2], p["g_ffn"], modc[4], modc[3],
            wr3, p["b_router"], tm=C)
        streams.append(dict(x=xc1, hf=hfc, idx=idxc, gate=gatec, rank=rankc, gt=modc[5], cnt=cnt))

    outs = _moe(streams, layer_idx, experts, g_final, final_norm)
    if update_ctx:
        return outs[0], outs[1]
    return outs[0], xc


def kernel(x, c, ctx, c_ctx, w_mod, b_mod, g_mix, g_ffn, w_in, w_out, lam_q1, lam_k1, lam_q2,
           lam_k2, g_sub, w_pool, s_pool, g_sgu, w_sgu, b_sgu, rpb, w_router, b_router, w_gu,
           b_gu, w_down, b_down, g_final):
    B, S, D = x.shape
    cos_tab, sin_tab = _rope_tables(S)
    tabs = dict(cos=cos_tab, sin=sin_tab)
    c8 = jnp.concatenate([c, c_ctx[None, :], jnp.zeros((8 - B - 1, D), F32)], axis=0)
    xc = ctx
    experts = dict(w_gu=w_gu, b_gu=b_gu, w_down=w_down, b_down=b_down)
    for l in range(DEPTH):
        p = dict(w_mod=w_mod[l], b_mod=b_mod[l], g_mix=g_mix[l], g_ffn=g_ffn[l], w_in=w_in[l],
                 w_out=w_out[l], lam_q1=lam_q1[l], lam_k1=lam_k1[l], lam_q2=lam_q2[l],
                 lam_k2=lam_k2[l], g_sub=g_sub[l], w_pool=w_pool[l], s_pool=s_pool[l],
                 g_sgu=g_sgu[l], w_sgu=w_sgu[l], b_sgu=b_sgu[l], rpb=rpb[l],
                 w_router=w_router[l], b_router=b_router[l])
        x, xc = _layer(x, xc, c8, p, experts, l, l < DEPTH - 1, tabs, g_final, l == DEPTH - 1)
    return x
```
